```python
import math
import jax, jax.numpy as jnp
from jax import lax
import numpy as np

D_MODEL = 1024
BATCH = 8
SEQ = 4096
DEPTH = 4
DEC_BATCH = 16
DEC_SEQ = 4096
PAST_LEN = 128

D_ATT = D_MODEL // 2
N_ATT_HEADS = 4
ATT_HEAD_DIM = D_ATT // N_ATT_HEADS
QK_DIM = ATT_HEAD_DIM // 2
D_GMLP = D_MODEL - D_ATT
N_GMLP_GROUPS = 4
GMLP_GROUP_DIM = D_GMLP // N_GMLP_GROUPS
CHUNK = 128
Q_BLOCK = 128
EPS = 1e-6
D_IN = 4 * D_ATT + 3 * D_GMLP
SPLITS = (D_ATT, 2 * D_ATT, 3 * D_ATT, 4 * D_ATT,
          4 * D_ATT + D_GMLP, 4 * D_ATT + 2 * D_GMLP)

kernel_name = "hymba_diffattn_gmlp_encoder"


def rmsnorm(x, g):
    xf = x.astype(jnp.float32)
    ms = jnp.mean(xf * xf, axis=-1, keepdims=True)
    return (xf * lax.rsqrt(ms + EPS) * g.astype(jnp.float32)).astype(x.dtype)


def alibi_slopes(n):
    return jnp.asarray(np.array([2.0 ** (-8.0 * (i + 1) / n) for i in range(n)], dtype=np.float32))


def lambda_init_fn(layer_idx):
    return 0.8 - 0.6 * math.exp(-0.3 * layer_idx)


def diff_attention(q, k, v, lam, lam_init, sub_g):
    B, S = q.shape[0], q.shape[1]
    nblk = S // Q_BLOCK
    scale = QK_DIM ** -0.5
    slopes = alibi_slopes(N_ATT_HEADS)
    pos = jnp.arange(S)
    qb = q.reshape(B, nblk, Q_BLOCK, N_ATT_HEADS, 2, QK_DIM).transpose(1, 0, 2, 3, 4, 5)

    def one_block(args):
        q_blk, blk = args
        t = blk * Q_BLOCK + jnp.arange(Q_BLOCK)
        dist = jnp.abs(t[:, None] - pos[None, :]).astype(jnp.float32)
        bias = -slopes[:, None, None] * dist
        s = jnp.einsum('bqhcd,bshcd->bhcqs', q_blk, k).astype(jnp.float32) * scale
        p = jax.nn.softmax(s + bias[None, :, None], axis=-1)
        a = p[:, :, 0] - lam * p[:, :, 1]
        return jnp.einsum('bhqs,bshe->bqhe', a.astype(v.dtype), v)

    o = lax.map(one_block, (qb, jnp.arange(nblk)))
    o = o.transpose(1, 0, 2, 3, 4).reshape(B, S, N_ATT_HEADS, ATT_HEAD_DIM)
    o = rmsnorm(o, sub_g) * (1.0 - lam_init)
    return o.reshape(B, S, D_ATT)


def spatial_gating(u, vg, vnorm_g, w_s, b_s):
    B, S = u.shape[0], u.shape[1]
    nc = S // CHUNK
    vn = rmsnorm(vg, vnorm_g).reshape(B, nc, CHUNK, N_GMLP_GROUPS, GMLP_GROUP_DIM)
    sv = jnp.einsum('gts,bcsgd->bctgd', w_s, vn) + b_s.T[None, None, :, :, None]
    return u * sv.reshape(B, S, D_GMLP)


def hybrid_layer(x, l, norm_g, w_in, lambda_qk, subln_g, vnorm_g, w_s, b_s, w_out):
    B, S = x.shape[0], x.shape[1]
    h = rmsnorm(x, norm_g[l])
    z = jnp.einsum('bsd,de->bse', h, w_in[l])
    zq, zk, zv, g_att, u, vg, g_gm = jnp.split(z, SPLITS, axis=-1)
    q = zq.reshape(B, S, N_ATT_HEADS, 2, QK_DIM)
    k = zk.reshape(B, S, N_ATT_HEADS, 2, QK_DIM)
    v = zv.reshape(B, S, N_ATT_HEADS, ATT_HEAD_DIM)
    lam_init = lambda_init_fn(l)
    lq = lambda_qk[l].astype(jnp.float32)
    lam = jnp.exp(jnp.sum(lq[0] * lq[1])) - jnp.exp(jnp.sum(lq[2] * lq[3])) + lam_init
    att = diff_attention(q, k, v, lam, lam_init, subln_g[l]) * jax.nn.silu(g_att)
    sgu = spatial_gating(u, vg, vnorm_g[l], w_s[l], b_s[l]) * jax.nn.silu(g_gm)
    y = jnp.einsum('bse,ed->bsd', jnp.concatenate([att, sgu], axis=-1), w_out[l])
    return x + y


def trunk(x, norm_g, w_in, lambda_qk, subln_g, vnorm_g, w_s, b_s, w_out, final_g):
    for l in range(DEPTH):
        x = hybrid_layer(x, l, norm_g, w_in, lambda_qk, subln_g, vnorm_g, w_s, b_s, w_out)
    return rmsnorm(x, final_g)


def setup_inputs(seed: int = 0) -> dict:
    key = jax.random.key(seed)
    ks = jax.random.split(key, 12)
    f32 = jnp.float32
    x_prompt = jax.random.normal(ks[0], (BATCH, SEQ, D_MODEL), f32)
    x_sample = jax.random.normal(ks[1], (DEC_BATCH, DEC_SEQ, D_MODEL), f32)
    norm_g = 1.0 + 0.02 * jax.random.normal(ks[2], (DEPTH, D_MODEL), f32)
    w_in = jax.random.normal(ks[3], (DEPTH, D_MODEL, D_IN), f32) * D_MODEL ** -0.5
    lambda_qk = 0.1 * jax.random.normal(ks[4], (DEPTH, 4, QK_DIM), f32)
    subln_g = 1.0 + 0.02 * jax.random.normal(ks[5], (DEPTH, ATT_HEAD_DIM), f32)
    vnorm_g = 1.0 + 0.02 * jax.random.normal(ks[6], (DEPTH, D_GMLP), f32)
    w_s = jax.random.normal(ks[7], (DEPTH, N_GMLP_GROUPS, CHUNK, CHUNK), f32) * CHUNK ** -0.5
    b_s = 1.0 + 0.1 * jax.random.normal(ks[8], (DEPTH, N_GMLP_GROUPS, CHUNK), f32)
    w_out = jax.random.normal(ks[9], (DEPTH, D_MODEL, D_MODEL), f32) * D_MODEL ** -0.5
    final_g = 1.0 + 0.02 * jax.random.normal(ks[10], (D_MODEL,), f32)
    return {"x_prompt": x_prompt, "x_sample": x_sample, "norm_g": norm_g, "w_in": w_in,
            "lambda_qk": lambda_qk, "subln_g": subln_g, "vnorm_g": vnorm_g, "w_s": w_s,
            "b_s": b_s, "w_out": w_out, "final_g": final_g}


def reference(x_prompt, x_sample, norm_g, w_in, lambda_qk, subln_g, vnorm_g, w_s, b_s, w_out, final_g):
    y_prompt = trunk(x_prompt, norm_g, w_in, lambda_qk, subln_g, vnorm_g, w_s, b_s, w_out, final_g)
    y_sample = trunk(x_sample, norm_g, w_in, lambda_qk, subln_g, vnorm_g, w_s, b_s, w_out, final_g)
    return (y_prompt, y_sample)
```

```python
import functools
import math

import jax
import jax.numpy as jnp
from jax import lax
from jax.experimental import pallas as pl
from jax.experimental.pallas import tpu as pltpu

D_MODEL = 1024
DEPTH = 4
D_ATT = 512
N_HEADS = 4
HEAD_DIM = 128
QK_DIM = 64
D_GMLP = 512
N_GROUPS = 4
GROUP_DIM = 128
CHUNK = 128
D_IN = 4 * D_ATT + 3 * D_GMLP
EPS = 1e-6

LANES = 128
VMEM_LIMIT = 56 * 1024 * 1024

TM_PROJ = 512
TQ = 256
TK = 512

F32 = jnp.float32
BF16 = jnp.bfloat16
NT_DIMS = (((1,), (1,)), ((), ()))


def _lambda_init(layer):
    return 0.8 - 0.6 * math.exp(-0.3 * layer)


def _in_proj_kernel(x_ref, g_ref, w_ref, z_ref):
    x = x_ref[...]
    ms = jnp.mean(x * x, axis=-1, keepdims=True)
    h = (x * lax.rsqrt(ms + EPS) * g_ref[...]).astype(BF16)
    for n0 in range(0, D_IN, 512):
        z_ref[:, n0:n0 + 512] = jnp.dot(
            h, w_ref[:, n0:n0 + 512], preferred_element_type=F32).astype(BF16)


def _in_proj(x2, g, w):
    n = x2.shape[0]
    return pl.pallas_call(
        _in_proj_kernel,
        grid=(n // TM_PROJ,),
        in_specs=[
            pl.BlockSpec((TM_PROJ, D_MODEL), lambda r: (r, 0)),
            pl.BlockSpec((1, D_MODEL), lambda r: (0, 0)),
            pl.BlockSpec((D_MODEL, D_IN), lambda r: (0, 0)),
        ],
        out_specs=pl.BlockSpec((TM_PROJ, D_IN), lambda r: (r, 0)),
        out_shape=jax.ShapeDtypeStruct((n, D_IN), BF16),
        compiler_params=pltpu.CompilerParams(
            dimension_semantics=("parallel",), vmem_limit_bytes=VMEM_LIMIT),
        name="in_proj",
    )(x2, g, w)


def _attn_kernel(slope_ref, q_ref, k_ref, v_ref, g_ref, lq_ref, sg_ref, o_ref,
                 qq_ref, bias_ref, m_ref, l_ref, acc_ref, *, seq, lam_init):
    h = pl.program_id(1)
    i = pl.program_id(2)
    n_bias = (2 * seq - TQ) // LANES

    @pl.when(i == 0)
    def _():
        slope = slope_ref[h]
        t = lax.broadcasted_iota(jnp.int32, (TQ, LANES), 0) + (seq - TQ)
        x = lax.broadcasted_iota(jnp.int32, (TQ, LANES), 1)
        base = (t - x).astype(F32)

        def fill(u, carry):
            d = base - (u * LANES).astype(F32)
            bias_ref[u] = -slope * jnp.abs(d)
            return carry

        lax.fori_loop(0, n_bias, fill, 0)

    q = q_ref[...]
    lane = lax.broadcasted_iota(jnp.int32, (TQ, HEAD_DIM), 1)
    zero = jnp.zeros_like(q)
    scale = jnp.asarray(QK_DIM ** -0.5, BF16)
    qq_ref[0:TQ, :] = jnp.where(lane < QK_DIM, q, zero) * scale
    qq_ref[TQ:2 * TQ, :] = jnp.where(lane >= QK_DIM, q, zero) * scale

    m_ref[...] = jnp.full(m_ref.shape, -jnp.inf, F32)
    l_ref[...] = jnp.zeros(l_ref.shape, F32)
    acc_ref[...] = jnp.zeros(acc_ref.shape, F32)

    ub0 = (seq - TQ) // LANES - i * (TQ // LANES)

    def body(c, carry):
        k0 = pl.multiple_of(c * TK, TK)
        kc = k_ref[pl.ds(k0, TK), :]
        vc = v_ref[pl.ds(k0, TK), :]
        s = lax.dot_general(qq_ref[...], kc, NT_DIMS, preferred_element_type=F32)
        ub = ub0 + c * (TK // LANES)
        bias = jnp.concatenate([bias_ref[ub + r] for r in range(TK // LANES)], axis=1)
        s = s + jnp.concatenate([bias, bias], axis=0)
        m_prev = m_ref[...]
        m_new = jnp.maximum(m_prev, jnp.max(s, axis=-1, keepdims=True))
        alpha = jnp.exp(m_prev - m_new)
        p = jnp.exp(s - m_new)
        l_ref[...] = alpha * l_ref[...] + jnp.sum(p, axis=-1, keepdims=True)
        acc_ref[...] = alpha * acc_ref[...] + jnp.dot(
            p.astype(BF16), vc, preferred_element_type=F32)
        m_ref[...] = m_new
        return carry

    lax.fori_loop(0, seq // TK, body, 0)

    lq = lq_ref[...]
    lam = (jnp.exp(jnp.sum(lq[0:1] * lq[1:2], axis=-1, keepdims=True))
           - jnp.exp(jnp.sum(lq[2:3] * lq[3:4], axis=-1, keepdims=True)) + lam_init)
    o = acc_ref[...] / l_ref[...]
    o = o[0:TQ] - lam * o[TQ:2 * TQ]
    ms = jnp.mean(o * o, axis=-1, keepdims=True)
    o = o * lax.rsqrt(ms + EPS) * sg_ref[...] * (1.0 - lam_init)
    g = g_ref[...].astype(F32)
    o_ref[...] = (o * (g * jax.nn.sigmoid(g))).astype(BF16)


def _attention(z, lq, sg, slopes, *, batch, seq, layer):
    n = batch * seq
    nq = seq // TQ
    kernel = functools.partial(_attn_kernel, seq=seq, lam_init=_lambda_init(layer))
    n_bias = (2 * seq - TQ) // LANES
    return pl.pallas_call(
        kernel,
        grid=(batch, N_HEADS, nq),
        in_specs=[
            pl.BlockSpec(memory_space=pltpu.SMEM),
            pl.BlockSpec((TQ, HEAD_DIM), lambda b, h, i: (b * nq + i, h)),
            pl.BlockSpec((seq, HEAD_DIM), lambda b, h, i: (b, N_HEADS + h)),
            pl.BlockSpec((seq, HEAD_DIM), lambda b, h, i: (b, 2 * N_HEADS + h)),
            pl.BlockSpec((TQ, HEAD_DIM), lambda b, h, i: (b * nq + i, 3 * N_HEADS + h)),
            pl.BlockSpec((4, QK_DIM), lambda b, h, i: (0, 0)),
            pl.BlockSpec((1, HEAD_DIM), lambda b, h, i: (0, 0)),
        ],
        out_specs=pl.BlockSpec((TQ, HEAD_DIM), lambda b, h, i: (b * nq + i, h)),
        out_shape=jax.ShapeDtypeStruct((n, D_ATT), BF16),
        scratch_shapes=[
            pltpu.VMEM((2 * TQ, HEAD_DIM), BF16),
            pltpu.VMEM((n_bias, TQ, LANES), F32),
            pltpu.VMEM((2 * TQ, 1), F32),
            pltpu.VMEM((2 * TQ, 1), F32),
            pltpu.VMEM((2 * TQ, HEAD_DIM), F32),
        ],
        compiler_params=pltpu.CompilerParams(
            dimension_semantics=("parallel", "parallel", "arbitrary"),
            vmem_limit_bytes=VMEM_LIMIT),
        name="diff_attn",
    )(slopes, z, z, z, z, lq, sg)


def _out_proj_kernel(x_ref, att_ref, u_ref, vg_ref, gg_ref, vng_ref, ws_ref, bs_ref,
                     wo_ref, fg_ref, o_ref, sgu_ref, *, final):
    vg = vg_ref[...].astype(F32)
    ms = jnp.mean(vg * vg, axis=-1, keepdims=True)
    vn = (vg * lax.rsqrt(ms + EPS) * vng_ref[...]).astype(BF16)
    gg = gg_ref[...].astype(F32)
    gate = u_ref[...].astype(F32) * (gg * jax.nn.sigmoid(gg))
    for c in range(TM_PROJ // CHUNK):
        r0 = c * CHUNK
        for g in range(N_GROUPS):
            c0 = g * GROUP_DIM
            sv = jnp.dot(ws_ref[g], vn[r0:r0 + CHUNK, c0:c0 + GROUP_DIM],
                         preferred_element_type=F32) + bs_ref[g]
            sgu_ref[r0:r0 + CHUNK, c0:c0 + GROUP_DIM] = (
                gate[r0:r0 + CHUNK, c0:c0 + GROUP_DIM] * sv).astype(BF16)
    y = jnp.dot(att_ref[...], wo_ref[0:D_ATT, :], preferred_element_type=F32)
    y = y + jnp.dot(sgu_ref[...], wo_ref[D_ATT:D_MODEL, :], preferred_element_type=F32)
    o = x_ref[...] + y
    if final:
        ms = jnp.mean(o * o, axis=-1, keepdims=True)
        o = o * lax.rsqrt(ms + EPS) * fg_ref[...]
    o_ref[...] = o


def _out_proj(x2, att, z, vng, ws, bs, wo, fg, *, final):
    n = x2.shape[0]
    kernel = functools.partial(_out_proj_kernel, final=final)
    u_blk = 4 * D_ATT // D_GMLP
    return pl.pallas_call(
        kernel,
        grid=(n // TM_PROJ,),
        in_specs=[
            pl.BlockSpec((TM_PROJ, D_MODEL), lambda r: (r, 0)),
            pl.BlockSpec((TM_PROJ, D_ATT), lambda r: (r, 0)),
            pl.BlockSpec((TM_PROJ, D_GMLP), lambda r: (r, u_blk)),
            pl.BlockSpec((TM_PROJ, D_GMLP), lambda r: (r, u_blk + 1)),
            pl.BlockSpec((TM_PROJ, D_GMLP), lambda r: (r, u_blk + 2)),
            pl.BlockSpec((1, D_GMLP), lambda r: (0, 0)),
            pl.BlockSpec((N_GROUPS, CHUNK, CHUNK), lambda r: (0, 0, 0)),
            pl.BlockSpec((N_GROUPS, CHUNK, 1), lambda r: (0, 0, 0)),
            pl.BlockSpec((D_MODEL, D_MODEL), lambda r: (0, 0)),
            pl.BlockSpec((1, D_MODEL), lambda r: (0, 0)),
        ],
        out_specs=pl.BlockSpec((TM_PROJ, D_MODEL), lambda r: (r, 0)),
        out_shape=jax.ShapeDtypeStruct((n, D_MODEL), F32),
        scratch_shapes=[pltpu.VMEM((TM_PROJ, D_GMLP), BF16)],
        compiler_params=pltpu.CompilerParams(
            dimension_semantics=("parallel",), vmem_limit_bytes=VMEM_LIMIT),
        name="out_proj",
    )(x2, att, z, z, z, vng, ws, bs, wo, fg)


def _trunk(x, norm_g, w_in, lambda_qk, subln_g, vnorm_g, w_s, b_s, w_out, final_g, slopes):
    batch, seq, _ = x.shape
    x2 = x.reshape(batch * seq, D_MODEL)
    fg = final_g.reshape(1, D_MODEL)
    for l in range(DEPTH):
        z = _in_proj(x2, norm_g[l].reshape(1, D_MODEL), w_in[l])
        att = _attention(z, lambda_qk[l], subln_g[l].reshape(1, HEAD_DIM), slopes,
                         batch=batch, seq=seq, layer=l)
        x2 = _out_proj(x2, att, z, vnorm_g[l].reshape(1, D_GMLP), w_s[l],
                       b_s[l].reshape(N_GROUPS, CHUNK, 1), w_out[l], fg,
                       final=(l == DEPTH - 1))
    return x2.reshape(batch, seq, D_MODEL)


def kernel(x_prompt, x_sample, norm_g, w_in, lambda_qk, subln_g, vnorm_g, w_s, b_s, w_out, final_g):
    slopes = jnp.asarray([2.0 ** (-8.0 * (i + 1) / N_HEADS) for i in range(N_HEADS)], F32)
    w_in_b = w_in.astype(BF16)
    w_s_b = w_s.astype(BF16)
    w_out_b = w_out.astype(BF16)
    args = (norm_g, w_in_b, lambda_qk, subln_g, vnorm_g, w_s_b, b_s, w_out_b, final_g, slopes)
    return (_trunk(x_prompt, *args), _trunk(x_sample, *args))
```

```python
import functools
import math

import jax
import jax.numpy as jnp
from jax import lax
from jax.experimental import pallas as pl
from jax.experimental.pallas import tpu as pltpu

D_MODEL = 1024
DEPTH = 4
D_ATT = 512
N_HEADS = 4
HEAD_DIM = 128
QK_DIM = 64
D_GMLP = 512
N_GROUPS = 4
GROUP_DIM = 128
CHUNK = 128
D_IN = 4 * D_ATT + 3 * D_GMLP
EPS = 1e-6
LOG2E = math.log2(math.e)

LANES = 128
VMEM_LIMIT = 56 * 1024 * 1024

TM_PROJ = 512
TQ = 256
TK = 512

F32 = jnp.float32
BF16 = jnp.bfloat16
NT_DIMS = (((1,), (1,)), ((), ()))


def _lambda_init(layer):
    return 0.8 - 0.6 * math.exp(-0.3 * layer)


def _in_proj_kernel(x_ref, g_ref, w_ref, z_ref):
    x = x_ref[...]
    ms = jnp.mean(x * x, axis=-1, keepdims=True)
    h = (x * lax.rsqrt(ms + EPS) * g_ref[...]).astype(BF16)
    for n0 in range(0, D_IN, D_ATT):
        zc = jnp.dot(h, w_ref[:, n0:n0 + D_ATT], preferred_element_type=F32)
        if n0 == D_ATT:
            zc = zc * LOG2E
        z_ref[:, n0:n0 + D_ATT] = zc.astype(BF16)


def _in_proj(x2, g, w):
    n = x2.shape[0]
    return pl.pallas_call(
        _in_proj_kernel,
        grid=(n // TM_PROJ,),
        in_specs=[
            pl.BlockSpec((TM_PROJ, D_MODEL), lambda r: (r, 0)),
            pl.BlockSpec((1, D_MODEL), lambda r: (0, 0)),
            pl.BlockSpec((D_MODEL, D_IN), lambda r: (0, 0)),
        ],
        out_specs=pl.BlockSpec((TM_PROJ, D_IN), lambda r: (r, 0)),
        out_shape=jax.ShapeDtypeStruct((n, D_IN), BF16),
        compiler_params=pltpu.CompilerParams(
            dimension_semantics=("parallel",), vmem_limit_bytes=VMEM_LIMIT),
        name="in_proj",
    )(x2, g, w)


def _attn_kernel(slope_ref, q_ref, k_ref, v_ref, g_ref, lq_ref, sg_ref, o_ref,
                 qq_ref, bias_ref, s0_ref, s1_ref, mx_ref, mrep_ref, l_ref, acc_ref,
                 *, seq, lam_init):
    s_refs = (s0_ref, s1_ref)
    h = pl.program_id(1)
    nq = seq // TQ
    nk = seq // TK
    kslabs = TK // LANES
    n_bias = (2 * seq - TQ) // LANES

    slope = slope_ref[h] * LOG2E
    t = lax.broadcasted_iota(jnp.int32, (TQ, LANES), 0) + (seq - TQ)
    x = lax.broadcasted_iota(jnp.int32, (TQ, LANES), 1)
    base = (t - x).astype(F32)

    def fill(u, carry):
        d = base - (u * LANES).astype(F32)
        bias_ref[u] = -slope * jnp.abs(d)
        return carry

    lax.fori_loop(0, n_bias, fill, 0)

    def reset_stats():
        mx_ref[...] = jnp.full(mx_ref.shape, -jnp.inf, F32)
        l_ref[...] = jnp.zeros(l_ref.shape, F32)
        acc_ref[...] = jnp.zeros(acc_ref.shape, F32)

    def load_queries(j):
        q = q_ref[pl.ds(pl.multiple_of(j * TQ, TQ), TQ), :]
        lane = lax.broadcasted_iota(jnp.int32, (TQ, HEAD_DIM), 1)
        zero = jnp.zeros_like(q)
        scale = jnp.asarray(QK_DIM ** -0.5, BF16)
        qq_ref[0:TQ, :] = jnp.where(lane < QK_DIM, q, zero) * scale
        qq_ref[TQ:2 * TQ, :] = jnp.where(lane >= QK_DIM, q, zero) * scale

    def score_chunk(j, c, s_ref):
        k0 = pl.multiple_of(c * TK, TK)
        s = lax.dot_general(qq_ref[...], k_ref[pl.ds(k0, TK), :], NT_DIMS,
                            preferred_element_type=F32)
        ub = (seq - TQ) // LANES - j * (TQ // LANES) + c * kslabs
        bias = jnp.concatenate([bias_ref[ub + r] for r in range(kslabs)], axis=1)
        s = s + jnp.concatenate([bias, bias], axis=0)
        s_ref[c] = s
        mx = mx_ref[...]
        for r in range(kslabs):
            mx = jnp.maximum(mx, s[:, r * LANES:(r + 1) * LANES])
        mx_ref[...] = mx

    def prob_chunk(c, s_ref):
        k0 = pl.multiple_of(c * TK, TK)
        s = s_ref[c]
        m = mrep_ref[...]
        p = jnp.exp2(s - jnp.concatenate([m] * kslabs, axis=1))
        lsum = l_ref[...]
        for r in range(kslabs):
            lsum = lsum + p[:, r * LANES:(r + 1) * LANES]
        l_ref[...] = lsum
        acc_ref[...] += jnp.dot(p.astype(BF16), v_ref[pl.ds(k0, TK), :],
                                preferred_element_type=F32)

    def publish_max():
        m = jnp.max(mx_ref[...], axis=-1, keepdims=True)
        mrep_ref[...] = jnp.broadcast_to(m, mrep_ref.shape)

    def finalize(j):
        lq = lq_ref[...]
        lam = (jnp.exp(jnp.sum(lq[0:1] * lq[1:2], axis=-1, keepdims=True))
               - jnp.exp(jnp.sum(lq[2:3] * lq[3:4], axis=-1, keepdims=True)) + lam_init)
        o = acc_ref[...] / jnp.sum(l_ref[...], axis=-1, keepdims=True)
        o = o[0:TQ] - lam * o[TQ:2 * TQ]
        ms = jnp.mean(o * o, axis=-1, keepdims=True)
        o = o * lax.rsqrt(ms + EPS) * sg_ref[...] * (1.0 - lam_init)
        rows = pl.ds(pl.multiple_of(j * TQ, TQ), TQ)
        g = g_ref[rows, :].astype(F32)
        o_ref[rows, :] = (o * (g * jax.nn.sigmoid(g))).astype(BF16)

    reset_stats()
    load_queries(0)

    def first_scores(c, carry):
        score_chunk(0, c, s_refs[0])
        return carry

    lax.fori_loop(0, nk, first_scores, 0)
    publish_max()
    reset_stats()

    def block_step(j, parity):
        load_queries(j)
        cur, prev = s_refs[parity], s_refs[1 - parity]

        def both(c, inner):
            score_chunk(j, c, cur)
            prob_chunk(c, prev)
            return inner

        lax.fori_loop(0, nk, both, 0, unroll=4)
        finalize(j - 1)
        publish_max()
        reset_stats()

    def block_pair(t, carry):
        block_step(2 * t + 1, 1)
        block_step(2 * t + 2, 0)
        return carry

    lax.fori_loop(0, (nq - 1) // 2, block_pair, 0)
    if (nq - 1) % 2:
        block_step(nq - 1, (nq - 1) % 2)

    last = s_refs[(nq - 1) % 2]

    def last_probs(c, carry):
        prob_chunk(c, last)
        return carry

    lax.fori_loop(0, nk, last_probs, 0)
    finalize(nq - 1)


def _attention(z, lq, sg, slopes, *, batch, seq, layer):
    n = batch * seq
    kernel = functools.partial(_attn_kernel, seq=seq, lam_init=_lambda_init(layer))
    n_bias = (2 * seq - TQ) // LANES
    col = lambda off: (lambda b, h: (b, off + h))
    return pl.pallas_call(
        kernel,
        grid=(batch, N_HEADS),
        in_specs=[
            pl.BlockSpec(memory_space=pltpu.SMEM),
            pl.BlockSpec((seq, HEAD_DIM), col(0)),
            pl.BlockSpec((seq, HEAD_DIM), col(N_HEADS)),
            pl.BlockSpec((seq, HEAD_DIM), col(2 * N_HEADS)),
            pl.BlockSpec((seq, HEAD_DIM), col(3 * N_HEADS)),
            pl.BlockSpec((4, QK_DIM), lambda b, h: (0, 0)),
            pl.BlockSpec((1, HEAD_DIM), lambda b, h: (0, 0)),
        ],
        out_specs=pl.BlockSpec((seq, HEAD_DIM), col(0)),
        out_shape=jax.ShapeDtypeStruct((n, D_ATT), BF16),
        scratch_shapes=[
            pltpu.VMEM((2 * TQ, HEAD_DIM), BF16),
            pltpu.VMEM((n_bias, TQ, LANES), F32),
            pltpu.VMEM((seq // TK, 2 * TQ, TK), F32),
            pltpu.VMEM((seq // TK, 2 * TQ, TK), F32),
            pltpu.VMEM((2 * TQ, LANES), F32),
            pltpu.VMEM((2 * TQ, LANES), F32),
            pltpu.VMEM((2 * TQ, LANES), F32),
            pltpu.VMEM((2 * TQ, HEAD_DIM), F32),
        ],
        compiler_params=pltpu.CompilerParams(
            dimension_semantics=("parallel", "parallel"),
            vmem_limit_bytes=VMEM_LIMIT),
        name="diff_attn",
    )(slopes, z, z, z, z, lq, sg)


def _out_proj_kernel(x_ref, att_ref, u_ref, vg_ref, gg_ref, vng_ref, ws_ref, bs_ref,
                     wo_ref, fg_ref, o_ref, sgu_ref, *, final):
    vg = vg_ref[...].astype(F32)
    ms = jnp.mean(vg * vg, axis=-1, keepdims=True)
    vn = (vg * lax.rsqrt(ms + EPS) * vng_ref[...]).astype(BF16)
    gg = gg_ref[...].astype(F32)
    gate = u_ref[...].astype(F32) * (gg * jax.nn.sigmoid(gg))
    for c in range(TM_PROJ // CHUNK):
        r0 = c * CHUNK
        for g in range(N_GROUPS):
            c0 = g * GROUP_DIM
            sv = jnp.dot(ws_ref[g], vn[r0:r0 + CHUNK, c0:c0 + GROUP_DIM],
                         preferred_element_type=F32) + bs_ref[g]
            sgu_ref[r0:r0 + CHUNK, c0:c0 + GROUP_DIM] = (
                gate[r0:r0 + CHUNK, c0:c0 + GROUP_DIM] * sv).astype(BF16)
    y = jnp.dot(att_ref[...], wo_ref[0:D_ATT, :], preferred_element_type=F32)
    y = y + jnp.dot(sgu_ref[...], wo_ref[D_ATT:D_MODEL, :], preferred_element_type=F32)
    o = x_ref[...] + y
    if final:
        ms = jnp.mean(o * o, axis=-1, keepdims=True)
        o = o * lax.rsqrt(ms + EPS) * fg_ref[...]
    o_ref[...] = o


def _out_proj(x2, att, z, vng, ws, bs, wo, fg, *, final):
    n = x2.shape[0]
    kernel = functools.partial(_out_proj_kernel, final=final)
    u_blk = 4 * D_ATT // D_GMLP
    return pl.pallas_call(
        kernel,
        grid=(n // TM_PROJ,),
        in_specs=[
            pl.BlockSpec((TM_PROJ, D_MODEL), lambda r: (r, 0)),
            pl.BlockSpec((TM_PROJ, D_ATT), lambda r: (r, 0)),
            pl.BlockSpec((TM_PROJ, D_GMLP), lambda r: (r, u_blk)),
            pl.BlockSpec((TM_PROJ, D_GMLP), lambda r: (r, u_blk + 1)),
            pl.BlockSpec((TM_PROJ, D_GMLP), lambda r: (r, u_blk + 2)),
            pl.BlockSpec((1, D_GMLP), lambda r: (0, 0)),
            pl.BlockSpec((N_GROUPS, CHUNK, CHUNK), lambda r: (0, 0, 0)),
            pl.BlockSpec((N_GROUPS, CHUNK, 1), lambda r: (0, 0, 0)),
            pl.BlockSpec((D_MODEL, D_MODEL), lambda r: (0, 0)),
            pl.BlockSpec((1, D_MODEL), lambda r: (0, 0)),
        ],
        out_specs=pl.BlockSpec((TM_PROJ, D_MODEL), lambda r: (r, 0)),
        out_shape=jax.ShapeDtypeStruct((n, D_MODEL), F32),
        scratch_shapes=[pltpu.VMEM((TM_PROJ, D_GMLP), BF16)],
        compiler_params=pltpu.CompilerParams(
            dimension_semantics=("parallel",), vmem_limit_bytes=VMEM_LIMIT),
        name="out_proj",
    )(x2, att, z, z, z, vng, ws, bs, wo, fg)


def _trunk(x, norm_g, w_in, lambda_qk, subln_g, vnorm_g, w_s, b_s, w_out, final_g, slopes):
    batch, seq, _ = x.shape
    x2 = x.reshape(batch * seq, D_MODEL)
    fg = final_g.reshape(1, D_MODEL)
    for l in range(DEPTH):
        z = _in_proj(x2, norm_g[l].reshape(1, D_MODEL), w_in[l])
        att = _attention(z, lambda_qk[l], subln_g[l].reshape(1, HEAD_DIM), slopes,
                         batch=batch, seq=seq, layer=l)
        x2 = _out_proj(x2, att, z, vnorm_g[l].reshape(1, D_GMLP), w_s[l],
                       b_s[l].reshape(N_GROUPS, CHUNK, 1), w_out[l], fg,
                       final=(l == DEPTH - 1))
    return x2.reshape(batch, seq, D_MODEL)


def kernel(x_prompt, x_sample, norm_g, w_in, lambda_qk, subln_g, vnorm_g, w_s, b_s, w_out, final_g):
    slopes = jnp.asarray([2.0 ** (-8.0 * (i + 1) / N_HEADS) for i in range(N_HEADS)], F32)
    w_in_b = w_in.astype(BF16)
    w_s_b = w_s.astype(BF16)
    w_out_b = w_out.astype(BF16)
    args = (norm_g, w_in_b, lambda_qk, subln_g, vnorm_g, w_s_b, b_s, w_out_b, final_g, slopes)
    return (_trunk(x_prompt, *args), _trunk(x_sample, *args))
```

```python
import functools
import math

import jax
import jax.numpy as jnp
from jax import lax
from jax.experimental import pallas as pl
from jax.experimental.pallas import tpu as pltpu

D_MODEL = 1024
DEPTH = 4
D_ATT = 512
N_HEADS = 4
HEAD_DIM = 128
QK_DIM = 64
D_GMLP = 512
N_GROUPS = 4
GROUP_DIM = 128
CHUNK = 128
D_IN = 4 * D_ATT + 3 * D_GMLP
EPS = 1e-6
LOG2E = math.log2(math.e)

LANES = 128
VMEM_LIMIT = 56 * 1024 * 1024

TM_PROJ = 512
TQ = 256
TK = 512

F32 = jnp.float32
BF16 = jnp.bfloat16
NT_DIMS = (((1,), (1,)), ((), ()))


def _lambda_init(layer):
    return 0.8 - 0.6 * math.exp(-0.3 * layer)


def _in_proj_kernel(x_ref, g_ref, w_ref, z_ref):
    x = x_ref[...]
    ms = jnp.mean(x * x, axis=-1, keepdims=True)
    h = (x * lax.rsqrt(ms + EPS) * g_ref[...]).astype(BF16)
    for n0 in range(0, D_IN, D_ATT):
        zc = jnp.dot(h, w_ref[:, n0:n0 + D_ATT], preferred_element_type=F32)
        if n0 == D_ATT:
            zc = zc * LOG2E
        z_ref[:, n0:n0 + D_ATT] = zc.astype(BF16)


def _in_proj(x2, g, w):
    n = x2.shape[0]
    return pl.pallas_call(
        _in_proj_kernel,
        grid=(n // TM_PROJ,),
        in_specs=[
            pl.BlockSpec((TM_PROJ, D_MODEL), lambda r: (r, 0)),
            pl.BlockSpec((1, D_MODEL), lambda r: (0, 0)),
            pl.BlockSpec((D_MODEL, D_IN), lambda r: (0, 0)),
        ],
        out_specs=pl.BlockSpec((TM_PROJ, D_IN), lambda r: (r, 0)),
        out_shape=jax.ShapeDtypeStruct((n, D_IN), BF16),
        compiler_params=pltpu.CompilerParams(
            dimension_semantics=("parallel",), vmem_limit_bytes=VMEM_LIMIT),
        name="in_proj",
    )(x2, g, w)


def _attn_kernel(slope_ref, q_ref, k_ref, v_ref, g_ref, lq_ref, sg_ref, o_ref,
                 qq_ref, bias_ref, vone_ref, s0_ref, s1_ref, mx_ref, mrep_ref, acc_ref,
                 *, seq, lam_init):
    s_refs = (s0_ref, s1_ref)
    h = pl.program_id(0)
    nq = seq // TQ
    nk = seq // TK
    kslabs = TK // LANES
    n_bias = (2 * seq - TQ) // LANES

    @pl.when(pl.program_id(1) == 0)
    def _():
        slope = slope_ref[h] * LOG2E
        t = lax.broadcasted_iota(jnp.int32, (TQ, LANES), 0) + (seq - TQ)
        x = lax.broadcasted_iota(jnp.int32, (TQ, LANES), 1)
        base = (t - x).astype(F32)

        def fill(u, carry):
            d = base - jnp.asarray(u * LANES, dtype=F32)
            bias_ref[u] = -slope * jnp.abs(d)
            return carry

        lax.fori_loop(0, n_bias, fill, 0)

    vone_ref[:, 0:HEAD_DIM] = v_ref[...]
    vone_ref[:, HEAD_DIM:2 * HEAD_DIM] = jnp.ones((seq, HEAD_DIM), BF16)

    def reset_stats():
        mx_ref[...] = jnp.full(mx_ref.shape, -jnp.inf, F32)
        acc_ref[...] = jnp.zeros(acc_ref.shape, F32)

    def load_queries(j):
        q = q_ref[pl.ds(pl.multiple_of(j * TQ, TQ), TQ), :]
        lane = lax.broadcasted_iota(jnp.int32, (TQ, HEAD_DIM), 1)
        zero = jnp.zeros_like(q)
        scale = jnp.asarray(QK_DIM ** -0.5, BF16)
        qq_ref[0:TQ, :] = jnp.where(lane < QK_DIM, q, zero) * scale
        qq_ref[TQ:2 * TQ, :] = jnp.where(lane >= QK_DIM, q, zero) * scale

    def score_chunk(j, c, s_ref):
        k0 = pl.multiple_of(c * TK, TK)
        s = lax.dot_general(qq_ref[...], k_ref[pl.ds(k0, TK), :], NT_DIMS,
                            preferred_element_type=F32)
        ub = (seq - TQ) // LANES - j * (TQ // LANES) + c * kslabs
        bias = jnp.concatenate([bias_ref[ub + r] for r in range(kslabs)], axis=1)
        s = s + jnp.concatenate([bias, bias], axis=0)
        s_ref[c] = s
        mx = mx_ref[...]
        for r in range(kslabs):
            mx = jnp.maximum(mx, s[:, r * LANES:(r + 1) * LANES])
        mx_ref[...] = mx

    def prob_chunk(c, s_ref):
        k0 = pl.multiple_of(c * TK, TK)
        s = s_ref[c]
        m = mrep_ref[...]
        p = jnp.exp2(s - jnp.concatenate([m] * kslabs, axis=1))
        acc_ref[...] += jnp.dot(p.astype(BF16), vone_ref[pl.ds(k0, TK), :],
                                preferred_element_type=F32)

    def publish_max():
        m = jnp.max(mx_ref[...], axis=-1, keepdims=True)
        mrep_ref[...] = jnp.broadcast_to(m, mrep_ref.shape)

    def finalize(j):
        lq = lq_ref[...]
        lam = (jnp.exp(jnp.sum(lq[0:1] * lq[1:2], axis=-1, keepdims=True))
               - jnp.exp(jnp.sum(lq[2:3] * lq[3:4], axis=-1, keepdims=True)) + lam_init)
        o = acc_ref[:, 0:HEAD_DIM] / acc_ref[:, HEAD_DIM:2 * HEAD_DIM]
        o = o[0:TQ] - lam * o[TQ:2 * TQ]
        ms = jnp.mean(o * o, axis=-1, keepdims=True)
        o = o * lax.rsqrt(ms + EPS) * sg_ref[...] * (1.0 - lam_init)
        rows = pl.ds(pl.multiple_of(j * TQ, TQ), TQ)
        g = g_ref[rows, :].astype(F32)
        o_ref[rows, :] = (o * (g * jax.nn.sigmoid(g))).astype(BF16)

    reset_stats()
    load_queries(0)

    def first_scores(c, carry):
        score_chunk(0, c, s_refs[0])
        return carry

    lax.fori_loop(0, nk, first_scores, 0, unroll=4)
    publish_max()
    reset_stats()

    def block_step(j, parity):
        load_queries(j)
        cur, prev = s_refs[parity], s_refs[1 - parity]

        def both(c, inner):
            score_chunk(j, c, cur)
            prob_chunk(c, prev)
            return inner

        lax.fori_loop(0, nk, both, 0, unroll=4)
        finalize(j - 1)
        publish_max()
        reset_stats()

    def block_pair(t, carry):
        block_step(2 * t + 1, 1)
        block_step(2 * t + 2, 0)
        return carry

    lax.fori_loop(0, (nq - 1) // 2, block_pair, 0)
    if (nq - 1) % 2:
        block_step(nq - 1, (nq - 1) % 2)

    last = s_refs[(nq - 1) % 2]

    def last_probs(c, carry):
        prob_chunk(c, last)
        return carry

    lax.fori_loop(0, nk, last_probs, 0, unroll=4)
    finalize(nq - 1)


def _attention(z, lq, sg, slopes, *, batch, seq, layer):
    n = batch * seq
    kernel = functools.partial(_attn_kernel, seq=seq, lam_init=_lambda_init(layer))
    n_bias = (2 * seq - TQ) // LANES
    col = lambda off: (lambda h, b: (b, off + h))
    return pl.pallas_call(
        kernel,
        grid=(N_HEADS, batch),
        in_specs=[
            pl.BlockSpec(memory_space=pltpu.SMEM),
            pl.BlockSpec((seq, HEAD_DIM), col(0)),
            pl.BlockSpec((seq, HEAD_DIM), col(N_HEADS)),
            pl.BlockSpec((seq, HEAD_DIM), col(2 * N_HEADS)),
            pl.BlockSpec((seq, HEAD_DIM), col(3 * N_HEADS)),
            pl.BlockSpec((4, QK_DIM), lambda h, b: (0, 0)),
            pl.BlockSpec((1, HEAD_DIM), lambda h, b: (0, 0)),
        ],
        out_specs=pl.BlockSpec((seq, HEAD_DIM), col(0)),
        out_shape=jax.ShapeDtypeStruct((n, D_ATT), BF16),
        scratch_shapes=[
            pltpu.VMEM((2 * TQ, HEAD_DIM), BF16),
            pltpu.VMEM((n_bias, TQ, LANES), F32),
            pltpu.VMEM((seq, 2 * HEAD_DIM), BF16),
            pltpu.VMEM((seq // TK, 2 * TQ, TK), F32),
            pltpu.VMEM((seq // TK, 2 * TQ, TK), F32),
            pltpu.VMEM((2 * TQ, LANES), F32),
            pltpu.VMEM((2 * TQ, LANES), F32),
            pltpu.VMEM((2 * TQ, 2 * HEAD_DIM), F32),
        ],
        compiler_params=pltpu.CompilerParams(
            dimension_semantics=("parallel", "arbitrary"),
            vmem_limit_bytes=VMEM_LIMIT),
        name="diff_attn",
    )(slopes, z, z, z, z, lq, sg)


def _out_proj_kernel(x_ref, att_ref, u_ref, vg_ref, gg_ref, vng_ref, ws_ref, bs_ref,
                     wo_ref, fg_ref, o_ref, sgu_ref, *, final):
    vg = vg_ref[...].astype(F32)
    ms = jnp.mean(vg * vg, axis=-1, keepdims=True)
    vn = (vg * lax.rsqrt(ms + EPS) * vng_ref[...]).astype(BF16)
    gg = gg_ref[...].astype(F32)
    gate = u_ref[...].astype(F32) * (gg * jax.nn.sigmoid(gg))
    for c in range(TM_PROJ // CHUNK):
        r0 = c * CHUNK
        for g in range(N_GROUPS):
            c0 = g * GROUP_DIM
            sv = jnp.dot(ws_ref[g], vn[r0:r0 + CHUNK, c0:c0 + GROUP_DIM],
                         preferred_element_type=F32) + bs_ref[g]
            sgu_ref[r0:r0 + CHUNK, c0:c0 + GROUP_DIM] = (
                gate[r0:r0 + CHUNK, c0:c0 + GROUP_DIM] * sv).astype(BF16)
    y = jnp.dot(att_ref[...], wo_ref[0:D_ATT, :], preferred_element_type=F32)
    y = y + jnp.dot(sgu_ref[...], wo_ref[D_ATT:D_MODEL, :], preferred_element_type=F32)
    o = x_ref[...] + y
    if final:
        ms = jnp.mean(o * o, axis=-1, keepdims=True)
        o = o * lax.rsqrt(ms + EPS) * fg_ref[...]
    o_ref[...] = o


def _out_proj(x2, att, z, vng, ws, bs, wo, fg, *, final):
    n = x2.shape[0]
    kernel = functools.partial(_out_proj_kernel, final=final)
    u_blk = 4 * D_ATT // D_GMLP
    return pl.pallas_call(
        kernel,
        grid=(n // TM_PROJ,),
        in_specs=[
            pl.BlockSpec((TM_PROJ, D_MODEL), lambda r: (r, 0)),
            pl.BlockSpec((TM_PROJ, D_ATT), lambda r: (r, 0)),
            pl.BlockSpec((TM_PROJ, D_GMLP), lambda r: (r, u_blk)),
            pl.BlockSpec((TM_PROJ, D_GMLP), lambda r: (r, u_blk + 1)),
            pl.BlockSpec((TM_PROJ, D_GMLP), lambda r: (r, u_blk + 2)),
            pl.BlockSpec((1, D_GMLP), lambda r: (0, 0)),
            pl.BlockSpec((N_GROUPS, CHUNK, CHUNK), lambda r: (0, 0, 0)),
            pl.BlockSpec((N_GROUPS, CHUNK, 1), lambda r: (0, 0, 0)),
            pl.BlockSpec((D_MODEL, D_MODEL), lambda r: (0, 0)),
            pl.BlockSpec((1, D_MODEL), lambda r: (0, 0)),
        ],
        out_specs=pl.BlockSpec((TM_PROJ, D_MODEL), lambda r: (r, 0)),
        out_shape=jax.ShapeDtypeStruct((n, D_MODEL), F32),
        scratch_shapes=[pltpu.VMEM((TM_PROJ, D_GMLP), BF16)],
        compiler_params=pltpu.CompilerParams(
            dimension_semantics=("parallel",), vmem_limit_bytes=VMEM_LIMIT),
        name="out_proj",
    )(x2, att, z, z, z, vng, ws, bs, wo, fg)


def _trunk(x, norm_g, w_in, lambda_qk, subln_g, vnorm_g, w_s, b_s, w_out, final_g, slopes):
    batch, seq, _ = x.shape
    x2 = x.reshape(batch * seq, D_MODEL)
    fg = final_g.reshape(1, D_MODEL)
    for l in range(DEPTH):
        z = _in_proj(x2, norm_g[l].reshape(1, D_MODEL), w_in[l])
        att = _attention(z, lambda_qk[l], subln_g[l].reshape(1, HEAD_DIM), slopes,
                         batch=batch, seq=seq, layer=l)
        x2 = _out_proj(x2, att, z, vnorm_g[l].reshape(1, D_GMLP), w_s[l],
                       b_s[l].reshape(N_GROUPS, CHUNK, 1), w_out[l], fg,
                       final=(l == DEPTH - 1))
    return x2.reshape(batch, seq, D_MODEL)


def kernel(x_prompt, x_sample, norm_g, w_in, lambda_qk, subln_g, vnorm_g, w_s, b_s, w_out, final_g):
    slopes = jnp.asarray([2.0 ** (-8.0 * (i + 1) / N_HEADS) for i in range(N_HEADS)], F32)
    w_in_b = w_in.astype(BF16)
    w_s_b = w_s.astype(BF16)
    w_out_b = w_out.astype(BF16)
    args = (norm_g, w_in_b, lambda_qk, subln_g, vnorm_g, w_s_b, b_s, w_out_b, final_g, slopes)
    return (_trunk(x_prompt, *args), _trunk(x_sample, *args))
```

```python
import functools
import math

import jax
import jax.numpy as jnp
from jax import lax
from jax.experimental import pallas as pl
from jax.experimental.pallas import tpu as pltpu

D_MODEL = 1024
DEPTH = 4
D_ATT = 512
N_HEADS = 4
HEAD_DIM = 128
QK_DIM = 64
D_GMLP = 512
N_GROUPS = 4
GROUP_DIM = 128
CHUNK = 128
D_IN = 4 * D_ATT + 3 * D_GMLP
EPS = 1e-6
LOG2E = math.log2(math.e)

LANES = 128
VMEM_LIMIT = 56 * 1024 * 1024

TM_PROJ = 512
TQ = 256
TK = 512
BAND_CHUNKS = 4
BAND_DIST = 3 * TQ + 1
BAND_MARGIN = 160.0
NORM_SLACK = 1.01
ALIBI_SLOPES = tuple(2.0 ** (-8.0 * (i + 1) / N_HEADS) for i in range(N_HEADS))
assert TK == 2 * TQ

F32 = jnp.float32
BF16 = jnp.bfloat16
NT_DIMS = (((1,), (1,)), ((), ()))


def _lambda_init(layer):
    return 0.8 - 0.6 * math.exp(-0.3 * layer)


def _in_proj_kernel(x_ref, g_ref, w_ref, z_ref):
    x = x_ref[...]
    ms = jnp.mean(x * x, axis=-1, keepdims=True)
    h = (x * lax.rsqrt(ms + EPS) * g_ref[...]).astype(BF16)
    for n0 in range(0, D_IN, D_ATT):
        zc = jnp.dot(h, w_ref[:, n0:n0 + D_ATT], preferred_element_type=F32)
        if n0 == D_ATT:
            zc = zc * LOG2E
        z_ref[:, n0:n0 + D_ATT] = zc.astype(BF16)


def _in_proj(x2, g, w):
    n = x2.shape[0]
    return pl.pallas_call(
        _in_proj_kernel,
        grid=(n // TM_PROJ,),
        in_specs=[
            pl.BlockSpec((TM_PROJ, D_MODEL), lambda r: (r, 0)),
            pl.BlockSpec((1, D_MODEL), lambda r: (0, 0)),
            pl.BlockSpec((D_MODEL, D_IN), lambda r: (0, 0)),
        ],
        out_specs=pl.BlockSpec((TM_PROJ, D_IN), lambda r: (r, 0)),
        out_shape=jax.ShapeDtypeStruct((n, D_IN), BF16),
        compiler_params=pltpu.CompilerParams(
            dimension_semantics=("parallel",), vmem_limit_bytes=VMEM_LIMIT),
        name="in_proj",
    )(x2, g, w)


def _attn_kernel(slope_ref, band_ref, q_ref, k_ref, v_ref, g_ref, lq_ref, sg_ref, o_ref,
                 qq_ref, bias_ref, vone_ref, s0_ref, s1_ref, mx_ref, mrep_ref, acc_ref,
                 *, seq, lam_init):
    s_refs = (s0_ref, s1_ref)
    h = pl.program_id(0)
    nq = seq // TQ
    nk = seq // TK
    kslabs = TK // LANES
    n_bias = (2 * seq - TQ) // LANES

    @pl.when(pl.program_id(1) == 0)
    def _():
        slope = slope_ref[h] * LOG2E
        t = lax.broadcasted_iota(jnp.int32, (TQ, LANES), 0) + (seq - TQ)
        x = lax.broadcasted_iota(jnp.int32, (TQ, LANES), 1)
        base = (t - x).astype(F32)

        def fill(u, carry):
            d = base - jnp.asarray(u * LANES, dtype=F32)
            bias_ref[u] = -slope * jnp.abs(d)
            return carry

        lax.fori_loop(0, n_bias, fill, 0)

    vone_ref[:, 0:HEAD_DIM] = v_ref[...]
    vone_ref[:, HEAD_DIM:2 * HEAD_DIM] = jnp.ones((seq, HEAD_DIM), BF16)

    def reset_stats():
        mx_ref[...] = jnp.full(mx_ref.shape, -jnp.inf, F32)
        acc_ref[...] = jnp.zeros(acc_ref.shape, F32)

    def load_queries(j):
        q = q_ref[pl.ds(pl.multiple_of(j * TQ, TQ), TQ), :]
        lane = lax.broadcasted_iota(jnp.int32, (TQ, HEAD_DIM), 1)
        zero = jnp.zeros_like(q)
        scale = jnp.asarray(QK_DIM ** -0.5, BF16)
        qq_ref[0:TQ, :] = jnp.where(lane < QK_DIM, q, zero) * scale
        qq_ref[TQ:2 * TQ, :] = jnp.where(lane >= QK_DIM, q, zero) * scale

    def score_chunk(j, c, s_ref):
        k0 = pl.multiple_of(c * TK, TK)
        s = lax.dot_general(qq_ref[...], k_ref[pl.ds(k0, TK), :], NT_DIMS,
                            preferred_element_type=F32)
        ub = (seq - TQ) // LANES - j * (TQ // LANES) + c * kslabs
        bias = jnp.concatenate([bias_ref[ub + r] for r in range(kslabs)], axis=1)
        s = s + jnp.concatenate([bias, bias], axis=0)
        s_ref[c] = s
        mx = mx_ref[...]
        for r in range(kslabs):
            mx = jnp.maximum(mx, s[:, r * LANES:(r + 1) * LANES])
        mx_ref[...] = mx

    def prob_chunk(c, s_ref):
        k0 = pl.multiple_of(c * TK, TK)
        s = s_ref[c]
        m = mrep_ref[...]
        p = jnp.exp2(s - jnp.concatenate([m] * kslabs, axis=1))
        acc_ref[...] += jnp.dot(p.astype(BF16), vone_ref[pl.ds(k0, TK), :],
                                preferred_element_type=F32)

    def publish_max():
        m = jnp.max(mx_ref[...], axis=-1, keepdims=True)
        mrep_ref[...] = jnp.broadcast_to(m, mrep_ref.shape)

    def finalize(j):
        lq = lq_ref[...]
        lam = (jnp.exp(jnp.sum(lq[0:1] * lq[1:2], axis=-1, keepdims=True))
               - jnp.exp(jnp.sum(lq[2:3] * lq[3:4], axis=-1, keepdims=True)) + lam_init)
        o = acc_ref[:, 0:HEAD_DIM] / acc_ref[:, HEAD_DIM:2 * HEAD_DIM]
        o = o[0:TQ] - lam * o[TQ:2 * TQ]
        ms = jnp.mean(o * o, axis=-1, keepdims=True)
        o = o * lax.rsqrt(ms + EPS) * sg_ref[...] * (1.0 - lam_init)
        rows = pl.ds(pl.multiple_of(j * TQ, TQ), TQ)
        g = g_ref[rows, :].astype(F32)
        o_ref[rows, :] = (o * (g * jax.nn.sigmoid(g))).astype(BF16)

    def for_chunks(band, fn, *blocks):
        if band:
            starts = [jnp.clip(lax.shift_right_arithmetic(jnp.asarray(j, jnp.int32) - 3, 1),
                               0, nk - BAND_CHUNKS) for j in blocks]
            for w in range(BAND_CHUNKS):
                fn(*[st + w for st in starts])
        else:
            def body(c, carry):
                fn(*([c] * len(blocks)))
                return carry

            lax.fori_loop(0, nk, body, 0, unroll=4)

    def run(band):
        reset_stats()
        load_queries(0)
        for_chunks(band, lambda c: score_chunk(0, c, s_refs[0]), 0)
        publish_max()
        reset_stats()

        def block_step(j, parity):
            load_queries(j)
            cur, prev = s_refs[parity], s_refs[1 - parity]

            def both(cj, cp):
                score_chunk(j, cj, cur)
                prob_chunk(cp, prev)

            for_chunks(band, both, j, j - 1)
            finalize(j - 1)
            publish_max()
            reset_stats()

        def block_pair(t, carry):
            block_step(2 * t + 1, 1)
            block_step(2 * t + 2, 0)
            return carry

        lax.fori_loop(0, (nq - 1) // 2, block_pair, 0)
        if (nq - 1) % 2:
            block_step(nq - 1, (nq - 1) % 2)

        last = s_refs[(nq - 1) % 2]
        for_chunks(band, lambda c: prob_chunk(c, last), nq - 1)
        finalize(nq - 1)

    if nk <= BAND_CHUNKS:
        run(False)
        return

    def band_is_exact():
        sel = (lax.broadcasted_iota(jnp.int32, (HEAD_DIM, 2 * HEAD_DIM), 0) // QK_DIM
               == lax.broadcasted_iota(jnp.int32, (HEAD_DIM, 2 * HEAD_DIM), 1) // HEAD_DIM
               ).astype(BF16)

        def max_half_norm2(ref):
            def body(r, best):
                x = ref[pl.ds(pl.multiple_of(r * TK, TK), TK), :]
                n2 = jnp.dot(x * x, sel, preferred_element_type=F32)
                return jnp.maximum(best, jnp.max(n2.reshape(TK // 8, 8, 2 * HEAD_DIM), axis=0))

            best = lax.fori_loop(0, seq // TK, body, jnp.zeros((8, 2 * HEAD_DIM), F32),
                                 unroll=True)
            return jnp.max(best, axis=0, keepdims=True)

        n2 = max_half_norm2(q_ref) * max_half_norm2(k_ref) * NORM_SLACK
        slope = jnp.full((1, 2 * HEAD_DIM), slope_ref[h], F32)
        gap = slope * (LOG2E * BAND_DIST) - BAND_MARGIN
        half = gap * (0.5 * QK_DIM ** 0.5)
        violated = jnp.where((n2 <= half * half) & (gap > 0.0), 0, 1)
        return jnp.max(violated)

    violated = lax.cond(band_ref[h] > 0, band_is_exact, lambda: jnp.int32(1))
    lax.cond(violated == 0, lambda: run(True), lambda: run(False))


def _attention(z, lq, sg, *, batch, seq, layer):
    n = batch * seq
    slopes = jnp.asarray(ALIBI_SLOPES, F32)
    band_heads = jnp.asarray(
        [int(sl * LOG2E * BAND_DIST > BAND_MARGIN) for sl in ALIBI_SLOPES], jnp.int32)
    kernel = functools.partial(_attn_kernel, seq=seq, lam_init=_lambda_init(layer))
    n_bias = (2 * seq - TQ) // LANES
    col = lambda off: (lambda h, b: (b, off + h))
    return pl.pallas_call(
        kernel,
        grid=(N_HEADS, batch),
        in_specs=[
            pl.BlockSpec(memory_space=pltpu.SMEM),
            pl.BlockSpec(memory_space=pltpu.SMEM),
            pl.BlockSpec((seq, HEAD_DIM), col(0)),
            pl.BlockSpec((seq, HEAD_DIM), col(N_HEADS)),
            pl.BlockSpec((seq, HEAD_DIM), col(2 * N_HEADS)),
            pl.BlockSpec((seq, HEAD_DIM), col(3 * N_HEADS)),
            pl.BlockSpec((4, QK_DIM), lambda h, b: (0, 0)),
            pl.BlockSpec((1, HEAD_DIM), lambda h, b: (0, 0)),
        ],
        out_specs=pl.BlockSpec((seq, HEAD_DIM), col(0)),
        out_shape=jax.ShapeDtypeStruct((n, D_ATT), BF16),
        scratch_shapes=[
            pltpu.VMEM((2 * TQ, HEAD_DIM), BF16),
            pltpu.VMEM((n_bias, TQ, LANES), F32),
            pltpu.VMEM((seq, 2 * HEAD_DIM), BF16),
            pltpu.VMEM((seq // TK, 2 * TQ, TK), F32),
            pltpu.VMEM((seq // TK, 2 * TQ, TK), F32),
            pltpu.VMEM((2 * TQ, LANES), F32),
            pltpu.VMEM((2 * TQ, LANES), F32),
            pltpu.VMEM((2 * TQ, 2 * HEAD_DIM), F32),
        ],
        compiler_params=pltpu.CompilerParams(
            dimension_semantics=("parallel", "arbitrary"),
            vmem_limit_bytes=VMEM_LIMIT),
        name="diff_attn",
    )(slopes, band_heads, z, z, z, z, lq, sg)


def _out_proj_kernel(x_ref, att_ref, u_ref, vg_ref, gg_ref, vng_ref, ws_ref, bs_ref,
                     wo_ref, fg_ref, o_ref, sgu_ref, *, final):
    vg = vg_ref[...].astype(F32)
    ms = jnp.mean(vg * vg, axis=-1, keepdims=True)
    vn = (vg * lax.rsqrt(ms + EPS) * vng_ref[...]).astype(BF16)
    gg = gg_ref[...].astype(F32)
    gate = u_ref[...].astype(F32) * (gg * jax.nn.sigmoid(gg))
    for c in range(TM_PROJ // CHUNK):
        r0 = c * CHUNK
        for g in range(N_GROUPS):
            c0 = g * GROUP_DIM
            sv = jnp.dot(ws_ref[g], vn[r0:r0 + CHUNK, c0:c0 + GROUP_DIM],
                         preferred_element_type=F32) + bs_ref[g]
            sgu_ref[r0:r0 + CHUNK, c0:c0 + GROUP_DIM] = (
                gate[r0:r0 + CHUNK, c0:c0 + GROUP_DIM] * sv).astype(BF16)
    y = jnp.dot(att_ref[...], wo_ref[0:D_ATT, :], preferred_element_type=F32)
    y = y + jnp.dot(sgu_ref[...], wo_ref[D_ATT:D_MODEL, :], preferred_element_type=F32)
    o = x_ref[...] + y
    if final:
        ms = jnp.mean(o * o, axis=-1, keepdims=True)
        o = o * lax.rsqrt(ms + EPS) * fg_ref[...]
    o_ref[...] = o


def _out_proj(x2, att, z, vng, ws, bs, wo, fg, *, final):
    n = x2.shape[0]
    kernel = functools.partial(_out_proj_kernel, final=final)
    u_blk = 4 * D_ATT // D_GMLP
    return pl.pallas_call(
        kernel,
        grid=(n // TM_PROJ,),
        in_specs=[
            pl.BlockSpec((TM_PROJ, D_MODEL), lambda r: (r, 0)),
            pl.BlockSpec((TM_PROJ, D_ATT), lambda r: (r, 0)),
            pl.BlockSpec((TM_PROJ, D_GMLP), lambda r: (r, u_blk)),
            pl.BlockSpec((TM_PROJ, D_GMLP), lambda r: (r, u_blk + 1)),
            pl.BlockSpec((TM_PROJ, D_GMLP), lambda r: (r, u_blk + 2)),
            pl.BlockSpec((1, D_GMLP), lambda r: (0, 0)),
            pl.BlockSpec((N_GROUPS, CHUNK, CHUNK), lambda r: (0, 0, 0)),
            pl.BlockSpec((N_GROUPS, CHUNK, 1), lambda r: (0, 0, 0)),
            pl.BlockSpec((D_MODEL, D_MODEL), lambda r: (0, 0)),
            pl.BlockSpec((1, D_MODEL), lambda r: (0, 0)),
        ],
        out_specs=pl.BlockSpec((TM_PROJ, D_MODEL), lambda r: (r, 0)),
        out_shape=jax.ShapeDtypeStruct((n, D_MODEL), F32),
        scratch_shapes=[pltpu.VMEM((TM_PROJ, D_GMLP), BF16)],
        compiler_params=pltpu.CompilerParams(
            dimension_semantics=("parallel",), vmem_limit_bytes=VMEM_LIMIT),
        name="out_proj",
    )(x2, att, z, z, z, vng, ws, bs, wo, fg)


def _trunk(x, norm_g, w_in, lambda_qk, subln_g, vnorm_g, w_s, b_s, w_out, final_g):
    batch, seq, _ = x.shape
    x2 = x.reshape(batch * seq, D_MODEL)
    fg = final_g.reshape(1, D_MODEL)
    for l in range(DEPTH):
        z = _in_proj(x2, norm_g[l].reshape(1, D_MODEL), w_in[l])
        att = _attention(z, lambda_qk[l], subln_g[l].reshape(1, HEAD_DIM),
                         batch=batch, seq=seq, layer=l)
        x2 = _out_proj(x2, att, z, vnorm_g[l].reshape(1, D_GMLP), w_s[l],
                       b_s[l].reshape(N_GROUPS, CHUNK, 1), w_out[l], fg,
                       final=(l == DEPTH - 1))
    return x2.reshape(batch, seq, D_MODEL)


def kernel(x_prompt, x_sample, norm_g, w_in, lambda_qk, subln_g, vnorm_g, w_s, b_s, w_out, final_g):
    w_in_b = w_in.astype(BF16)
    w_s_b = w_s.astype(BF16)
    w_out_b = w_out.astype(BF16)
    args = (norm_g, w_in_b, lambda_qk, subln_g, vnorm_g, w_s_b, b_s, w_out_b, final_g)
    return (_trunk(x_prompt, *args), _trunk(x_sample, *args))
```

```python
import functools
import math

import jax
import jax.numpy as jnp
from jax import lax
from jax.experimental import pallas as pl
from jax.experimental.pallas import tpu as pltpu

D_MODEL = 1024
DEPTH = 4
D_ATT = 512
N_HEADS = 4
HEAD_DIM = 128
QK_DIM = 64
D_GMLP = 512
N_GROUPS = 4
GROUP_DIM = 128
CHUNK = 128
D_IN = 4 * D_ATT + 3 * D_GMLP
EPS = 1e-6
LOG2E = math.log2(math.e)

LANES = 128
VMEM_LIMIT = 56 * 1024 * 1024

TM_PROJ = 512
TQ = 256
TK = 256
GROUP_CHUNKS = 8
BAND_SIDE = 3
BAND_CHUNKS = 2 * BAND_SIDE + 1
BAND_DIST = BAND_SIDE * TK + 1
BAND_MARGIN = 160.0
NORM_SLACK = 1.01
ALIBI_SLOPES = tuple(2.0 ** (-8.0 * (i + 1) / N_HEADS) for i in range(N_HEADS))
assert TK == TQ

F32 = jnp.float32
BF16 = jnp.bfloat16
NT_DIMS = (((1,), (1,)), ((), ()))


def _lambda_init(layer):
    return 0.8 - 0.6 * math.exp(-0.3 * layer)


def _in_proj_kernel(x_ref, g_ref, w_ref, z_ref):
    _in_proj_rows(x_ref[...], g_ref, w_ref, z_ref)


def _in_proj_rows(x, g_ref, w_ref, z_ref):
    ms = jnp.mean(x * x, axis=-1, keepdims=True)
    h = (x * lax.rsqrt(ms + EPS) * g_ref[...]).astype(BF16)
    for n0 in range(0, D_IN, D_ATT):
        zc = jnp.dot(h, w_ref[:, n0:n0 + D_ATT], preferred_element_type=F32)
        if n0 == D_ATT:
            zc = zc * LOG2E
        z_ref[:, n0:n0 + D_ATT] = zc.astype(BF16)


def _in_proj(x2, g, w):
    n = x2.shape[0]
    return pl.pallas_call(
        _in_proj_kernel,
        grid=(n // TM_PROJ,),
        in_specs=[
            pl.BlockSpec((TM_PROJ, D_MODEL), lambda r: (r, 0)),
            pl.BlockSpec((1, D_MODEL), lambda r: (0, 0)),
            pl.BlockSpec((D_MODEL, D_IN), lambda r: (0, 0)),
        ],
        out_specs=pl.BlockSpec((TM_PROJ, D_IN), lambda r: (r, 0)),
        out_shape=jax.ShapeDtypeStruct((n, D_IN), BF16),
        compiler_params=pltpu.CompilerParams(
            dimension_semantics=("parallel",), vmem_limit_bytes=VMEM_LIMIT),
        name="in_proj",
    )(x2, g, w)


def _attn_kernel(slope_ref, band_ref, q_ref, k_ref, v_ref, g_ref, lq_ref, sg_ref, o_ref,
                 qq_ref, bias_ref, vone_ref, s0_ref, s1_ref, mx_ref, mrep_ref, acc_ref,
                 *, seq, lam_init):
    s_refs = (s0_ref, s1_ref)
    h = pl.program_id(0)
    nq = seq // TQ
    nk = seq // TK
    group_chunks = max(d for d in range(1, GROUP_CHUNKS + 1) if nk % d == 0)
    kslabs = TK // LANES
    n_bias = (2 * seq - TQ) // LANES

    @pl.when(pl.program_id(1) == 0)
    def _():
        slope = slope_ref[h] * LOG2E
        t = lax.broadcasted_iota(jnp.int32, (TQ, LANES), 0) + (seq - TQ)
        x = lax.broadcasted_iota(jnp.int32, (TQ, LANES), 1)
        base = (t - x).astype(F32)

        def fill(u, carry):
            d = base - jnp.asarray(u * LANES, dtype=F32)
            bias_ref[u] = -slope * jnp.abs(d)
            return carry

        lax.fori_loop(0, n_bias, fill, 0)

    vone_ref[:, 0:HEAD_DIM] = v_ref[...]
    vone_ref[:, HEAD_DIM:2 * HEAD_DIM] = jnp.ones((seq, HEAD_DIM), BF16)

    def reset_stats():
        mx_ref[...] = jnp.full(mx_ref.shape, -jnp.inf, F32)
        acc_ref[...] = jnp.zeros(acc_ref.shape, F32)

    def load_queries(j):
        q = q_ref[pl.ds(pl.multiple_of(j * TQ, TQ), TQ), :]
        lane = lax.broadcasted_iota(jnp.int32, (TQ, HEAD_DIM), 1)
        zero = jnp.zeros_like(q)
        scale = jnp.asarray(QK_DIM ** -0.5, BF16)
        qq_ref[0:TQ, :] = jnp.where(lane < QK_DIM, q, zero) * scale
        qq_ref[TQ:2 * TQ, :] = jnp.where(lane >= QK_DIM, q, zero) * scale

    def score_chunk(j, c, s_ref):
        k0 = pl.multiple_of(c * TK, TK)
        s = lax.dot_general(qq_ref[...], k_ref[pl.ds(k0, TK), :], NT_DIMS,
                            preferred_element_type=F32)
        ub = (seq - TQ) // LANES - j * (TQ // LANES) + c * kslabs
        bias = jnp.concatenate([bias_ref[ub + r] for r in range(kslabs)], axis=1)
        s = s + jnp.concatenate([bias, bias], axis=0)
        s_ref[c] = s
        mx = mx_ref[...]
        for r in range(kslabs):
            mx = jnp.maximum(mx, s[:, r * LANES:(r + 1) * LANES])
        mx_ref[...] = mx

    def prob_chunk(c, s_ref):
        k0 = pl.multiple_of(c * TK, TK)
        s = s_ref[c]
        m = mrep_ref[...]
        p = jnp.exp2(s - jnp.concatenate([m] * kslabs, axis=1))
        acc_ref[...] += jnp.dot(p.astype(BF16), vone_ref[pl.ds(k0, TK), :],
                                preferred_element_type=F32)

    def publish_max():
        m = jnp.max(mx_ref[...], axis=-1, keepdims=True)
        mrep_ref[...] = jnp.broadcast_to(m, mrep_ref.shape)

    def finalize(j):
        lq = lq_ref[...]
        lam = (jnp.exp(jnp.sum(lq[0:1] * lq[1:2], axis=-1, keepdims=True))
               - jnp.exp(jnp.sum(lq[2:3] * lq[3:4], axis=-1, keepdims=True)) + lam_init)
        o = acc_ref[:, 0:HEAD_DIM] / acc_ref[:, HEAD_DIM:2 * HEAD_DIM]
        o = o[0:TQ] - lam * o[TQ:2 * TQ]
        ms = jnp.mean(o * o, axis=-1, keepdims=True)
        o = o * lax.rsqrt(ms + EPS) * sg_ref[...] * (1.0 - lam_init)
        rows = pl.ds(pl.multiple_of(j * TQ, TQ), TQ)
        g = g_ref[rows, :].astype(F32)
        o_ref[rows, :] = (o * (g * jax.nn.sigmoid(g))).astype(BF16)

    def for_chunks(band, fn, *blocks):
        if band:
            starts = [jnp.clip(jnp.asarray(j, jnp.int32) - BAND_SIDE, 0, nk - BAND_CHUNKS)
                      for j in blocks]
            for w in range(BAND_CHUNKS):
                fn(*[st + w for st in starts])
        else:
            def body(c, carry):
                fn(*([c] * len(blocks)))
                return carry

            lax.fori_loop(0, nk, body, 0, unroll=group_chunks)

    def run(band):
        reset_stats()
        load_queries(0)
        for_chunks(band, lambda c: score_chunk(0, c, s_refs[0]), 0)
        publish_max()
        reset_stats()

        def block_step(j, parity):
            load_queries(j)
            cur, prev = s_refs[parity], s_refs[1 - parity]

            def both(cj, cp):
                score_chunk(j, cj, cur)
                prob_chunk(cp, prev)

            if band:
                for_chunks(band, both, j, j - 1)
            else:
                def group(gi, carry):
                    for w in range(group_chunks):
                        both(gi * group_chunks + w, gi * group_chunks + w)
                    return carry

                lax.fori_loop(0, band_ref[N_HEADS] * (nk // group_chunks - 1), group, 0)
                for c in range(nk - group_chunks, nk):
                    both(c, c)
            finalize(j - 1)
            publish_max()
            reset_stats()

        def block_pair(t, carry):
            block_step(2 * t + 1, 1)
            block_step(2 * t + 2, 0)
            return carry

        lax.fori_loop(0, (nq - 1) // 2, block_pair, 0)
        if (nq - 1) % 2:
            block_step(nq - 1, (nq - 1) % 2)

        last = s_refs[(nq - 1) % 2]
        for_chunks(band, lambda c: prob_chunk(c, last), nq - 1)
        finalize(nq - 1)

    if nk <= BAND_CHUNKS:
        run(False)
        return

    def band_is_exact():
        sel = (lax.broadcasted_iota(jnp.int32, (HEAD_DIM, 2 * HEAD_DIM), 0) // QK_DIM
               == lax.broadcasted_iota(jnp.int32, (HEAD_DIM, 2 * HEAD_DIM), 1) // HEAD_DIM
               ).astype(BF16)

        def max_half_norm2(ref):
            def body(r, best):
                x = ref[pl.ds(pl.multiple_of(r * TK, TK), TK), :]
                n2 = jnp.dot(x * x, sel, preferred_element_type=F32)
                return jnp.maximum(best, jnp.max(n2.reshape(TK // 8, 8, 2 * HEAD_DIM), axis=0))

            best = lax.fori_loop(0, seq // TK, body, jnp.zeros((8, 2 * HEAD_DIM), F32),
                                 unroll=True)
            return jnp.max(best, axis=0, keepdims=True)

        n2 = max_half_norm2(q_ref) * max_half_norm2(k_ref) * NORM_SLACK
        slope = jnp.full((1, 2 * HEAD_DIM), slope_ref[h], F32)
        gap = slope * (LOG2E * BAND_DIST) - BAND_MARGIN
        half = gap * (0.5 * QK_DIM ** 0.5)
        violated = jnp.where((n2 <= half * half) & (gap > 0.0), 0, 1)
        return jnp.max(violated)

    violated = lax.cond(band_ref[h] > 0, band_is_exact, lambda: jnp.int32(1))
    lax.cond(violated == 0, lambda: run(True), lambda: run(False))


def _attention(z, lq, sg, *, batch, seq, layer):
    n = batch * seq
    slopes = jnp.asarray(ALIBI_SLOPES, F32)
    band_heads = jnp.asarray(
        [int(sl * LOG2E * BAND_DIST > BAND_MARGIN) for sl in ALIBI_SLOPES] + [1], jnp.int32)
    kernel = functools.partial(_attn_kernel, seq=seq, lam_init=_lambda_init(layer))
    n_bias = (2 * seq - TQ) // LANES
    col = lambda off: (lambda h, b: (b, off + h))
    return pl.pallas_call(
        kernel,
        grid=(N_HEADS, batch),
        in_specs=[
            pl.BlockSpec(memory_space=pltpu.SMEM),
            pl.BlockSpec(memory_space=pltpu.SMEM),
            pl.BlockSpec((seq, HEAD_DIM), col(0)),
            pl.BlockSpec((seq, HEAD_DIM), col(N_HEADS)),
            pl.BlockSpec((seq, HEAD_DIM), col(2 * N_HEADS)),
            pl.BlockSpec((seq, HEAD_DIM), col(3 * N_HEADS)),
            pl.BlockSpec((4, QK_DIM), lambda h, b: (0, 0)),
            pl.BlockSpec((1, HEAD_DIM), lambda h, b: (0, 0)),
        ],
        out_specs=pl.BlockSpec((seq, HEAD_DIM), col(0)),
        out_shape=jax.ShapeDtypeStruct((n, D_ATT), BF16),
        scratch_shapes=[
            pltpu.VMEM((2 * TQ, HEAD_DIM), BF16),
            pltpu.VMEM((n_bias, TQ, LANES), F32),
            pltpu.VMEM((seq, 2 * HEAD_DIM), BF16),
            pltpu.VMEM((seq // TK, 2 * TQ, TK), F32),
            pltpu.VMEM((seq // TK, 2 * TQ, TK), F32),
            pltpu.VMEM((2 * TQ, LANES), F32),
            pltpu.VMEM((2 * TQ, LANES), F32),
            pltpu.VMEM((2 * TQ, 2 * HEAD_DIM), F32),
        ],
        compiler_params=pltpu.CompilerParams(
            dimension_semantics=("parallel", "arbitrary"),
            vmem_limit_bytes=VMEM_LIMIT),
        name="diff_attn",
    )(slopes, band_heads, z, z, z, z, lq, sg)


def _out_rows(x_ref, att_ref, u_ref, vg_ref, gg_ref, vng_ref, ws_ref, bs_ref, wo_ref, sgu_ref):
    vg = vg_ref[...].astype(F32)
    ms = jnp.mean(vg * vg, axis=-1, keepdims=True)
    vn = (vg * lax.rsqrt(ms + EPS) * vng_ref[...]).astype(BF16)
    gg = gg_ref[...].astype(F32)
    gate = u_ref[...].astype(F32) * (gg * jax.nn.sigmoid(gg))
    for c in range(TM_PROJ // CHUNK):
        r0 = c * CHUNK
        for g in range(N_GROUPS):
            c0 = g * GROUP_DIM
            sv = jnp.dot(ws_ref[g], vn[r0:r0 + CHUNK, c0:c0 + GROUP_DIM],
                         preferred_element_type=F32) + bs_ref[g]
            sgu_ref[r0:r0 + CHUNK, c0:c0 + GROUP_DIM] = (
                gate[r0:r0 + CHUNK, c0:c0 + GROUP_DIM] * sv).astype(BF16)
    y = jnp.dot(att_ref[...], wo_ref[0:D_ATT, :], preferred_element_type=F32)
    y = y + jnp.dot(sgu_ref[...], wo_ref[D_ATT:D_MODEL, :], preferred_element_type=F32)
    return x_ref[...] + y


def _out_final_kernel(x_ref, att_ref, u_ref, vg_ref, gg_ref, vng_ref, ws_ref, bs_ref,
                      wo_ref, fg_ref, o_ref, sgu_ref):
    o = _out_rows(x_ref, att_ref, u_ref, vg_ref, gg_ref, vng_ref, ws_ref, bs_ref, wo_ref,
                  sgu_ref)
    ms = jnp.mean(o * o, axis=-1, keepdims=True)
    o_ref[...] = o * lax.rsqrt(ms + EPS) * fg_ref[...]


def _out_in_kernel(x_ref, att_ref, u_ref, vg_ref, gg_ref, vng_ref, ws_ref, bs_ref,
                   wo_ref, g_next_ref, w_next_ref, o_ref, z_ref, sgu_ref):
    o = _out_rows(x_ref, att_ref, u_ref, vg_ref, gg_ref, vng_ref, ws_ref, bs_ref, wo_ref,
                  sgu_ref)
    o_ref[...] = o
    _in_proj_rows(o, g_next_ref, w_next_ref, z_ref)


def _out_proj(x2, att, z, vng, ws, bs, wo, g, w_next=None):
    n = x2.shape[0]
    u_blk = 4 * D_ATT // D_GMLP
    rows = lambda width, blk: pl.BlockSpec((TM_PROJ, width), lambda r: (r, blk))
    whole = lambda shape: pl.BlockSpec(shape, lambda r: (0,) * len(shape))
    in_specs = [
        rows(D_MODEL, 0), rows(D_ATT, 0),
        rows(D_GMLP, u_blk), rows(D_GMLP, u_blk + 1), rows(D_GMLP, u_blk + 2),
        whole((1, D_GMLP)), whole((N_GROUPS, CHUNK, CHUNK)), whole((N_GROUPS, CHUNK, 1)),
        whole((D_MODEL, D_MODEL)), whole((1, D_MODEL)),
    ]
    operands = [x2, att, z, z, z, vng, ws, bs, wo, g]
    out_specs = rows(D_MODEL, 0)
    out_shape = jax.ShapeDtypeStruct((n, D_MODEL), F32)
    body = _out_final_kernel
    if w_next is not None:
        in_specs.append(whole((D_MODEL, D_IN)))
        operands.append(w_next)
        out_specs = (out_specs, rows(D_IN, 0))
        out_shape = (out_shape, jax.ShapeDtypeStruct((n, D_IN), BF16))
        body = _out_in_kernel
    return pl.pallas_call(
        body,
        grid=(n // TM_PROJ,),
        in_specs=in_specs,
        out_specs=out_specs,
        out_shape=out_shape,
        scratch_shapes=[pltpu.VMEM((TM_PROJ, D_GMLP), BF16)],
        compiler_params=pltpu.CompilerParams(
            dimension_semantics=("parallel",), vmem_limit_bytes=VMEM_LIMIT),
        name="out_proj" if w_next is None else "out_in_proj",
    )(*operands)


def _trunk(x, norm_g, w_in, lambda_qk, subln_g, vnorm_g, w_s, b_s, w_out, final_g):
    batch, seq, _ = x.shape
    x2 = x.reshape(batch * seq, D_MODEL)
    z = _in_proj(x2, norm_g[0].reshape(1, D_MODEL), w_in[0])
    for l in range(DEPTH):
        att = _attention(z, lambda_qk[l], subln_g[l].reshape(1, HEAD_DIM),
                         batch=batch, seq=seq, layer=l)
        layer_args = (x2, att, z, vnorm_g[l].reshape(1, D_GMLP), w_s[l],
                      b_s[l].reshape(N_GROUPS, CHUNK, 1), w_out[l])
        if l + 1 < DEPTH:
            x2, z = _out_proj(*layer_args, norm_g[l + 1].reshape(1, D_MODEL), w_in[l + 1])
        else:
            x2 = _out_proj(*layer_args, final_g.reshape(1, D_MODEL))
    return x2.reshape(batch, seq, D_MODEL)


def kernel(x_prompt, x_sample, norm_g, w_in, lambda_qk, subln_g, vnorm_g, w_s, b_s, w_out, final_g):
    w_in_b = w_in.astype(BF16)
    w_s_b = w_s.astype(BF16)
    w_out_b = w_out.astype(BF16)
    args = (norm_g, w_in_b, lambda_qk, subln_g, vnorm_g, w_s_b, b_s, w_out_b, final_g)
    return (_trunk(x_prompt, *args), _trunk(x_sample, *args))
```

```python
import functools
import math

import jax
import jax.numpy as jnp
from jax import lax
from jax.experimental import pallas as pl
from jax.experimental.pallas import tpu as pltpu

D_MODEL = 1024
DEPTH = 4
D_ATT = 512
N_HEADS = 4
HEAD_DIM = 128
QK_DIM = 64
D_GMLP = 512
N_GROUPS = 4
GROUP_DIM = 128
CHUNK = 128
D_IN = 4 * D_ATT + 3 * D_GMLP
EPS = 1e-6
LOG2E = math.log2(math.e)

LANES = 128
VMEM_LIMIT = 56 * 1024 * 1024

TM_PROJ = 512
TQ = 256
TK = 256
GROUP_CHUNKS = 16
BAND_SIDE = 3
BAND_CHUNKS = 2 * BAND_SIDE + 1
BAND_DIST = BAND_SIDE * TK + 1
BAND_MARGIN = 160.0
NORM_SLACK = 1.01
ALIBI_SLOPES = tuple(2.0 ** (-8.0 * (i + 1) / N_HEADS) for i in range(N_HEADS))
assert TK == TQ

F32 = jnp.float32
BF16 = jnp.bfloat16
NT_DIMS = (((1,), (1,)), ((), ()))


def _lambda_init(layer):
    return 0.8 - 0.6 * math.exp(-0.3 * layer)


def _in_proj_kernel(x_ref, g_ref, w_ref, z_ref):
    _in_proj_rows(x_ref[...], g_ref, w_ref, z_ref)


def _in_proj_rows(x, g_ref, w_ref, z_ref):
    ms = jnp.mean(x * x, axis=-1, keepdims=True)
    h = (x * lax.rsqrt(ms + EPS) * g_ref[...]).astype(BF16)
    for n0 in range(0, D_IN, D_ATT):
        zc = jnp.dot(h, w_ref[:, n0:n0 + D_ATT], preferred_element_type=F32)
        if n0 == D_ATT:
            zc = zc * LOG2E
        z_ref[:, n0:n0 + D_ATT] = zc.astype(BF16)


def _in_proj(x2, g, w):
    n = x2.shape[0]
    return pl.pallas_call(
        _in_proj_kernel,
        grid=(n // TM_PROJ,),
        in_specs=[
            pl.BlockSpec((TM_PROJ, D_MODEL), lambda r: (r, 0)),
            pl.BlockSpec((1, D_MODEL), lambda r: (0, 0)),
            pl.BlockSpec((D_MODEL, D_IN), lambda r: (0, 0)),
        ],
        out_specs=pl.BlockSpec((TM_PROJ, D_IN), lambda r: (r, 0)),
        out_shape=jax.ShapeDtypeStruct((n, D_IN), BF16),
        compiler_params=pltpu.CompilerParams(
            dimension_semantics=("parallel",), vmem_limit_bytes=VMEM_LIMIT),
        name="in_proj",
    )(x2, g, w)


def _attn_kernel(slope_ref, band_ref, q_ref, k_ref, v_ref, g_ref, lq_ref, sg_ref, o_ref,
                 qq_ref, bias_ref, vone_ref, s0_ref, s1_ref, mx_ref, mrep_ref, acc_ref,
                 *, seq, lam_init):
    s_refs = (s0_ref, s1_ref)
    h = pl.program_id(0)
    nq = seq // TQ
    nk = seq // TK
    group_chunks = max(d for d in range(1, GROUP_CHUNKS + 1) if nk % d == 0)
    kslabs = TK // LANES
    n_bias = (2 * seq - TQ) // LANES

    @pl.when(pl.program_id(1) == 0)
    def _():
        slope = slope_ref[h] * LOG2E
        t = lax.broadcasted_iota(jnp.int32, (TQ, LANES), 0) + (seq - TQ)
        x = lax.broadcasted_iota(jnp.int32, (TQ, LANES), 1)
        base = (t - x).astype(F32)

        def fill(u, carry):
            d = base - jnp.asarray(u * LANES, dtype=F32)
            bias_ref[u] = -slope * jnp.abs(d)
            return carry

        lax.fori_loop(0, n_bias, fill, 0)

    vone_ref[:, 0:HEAD_DIM] = v_ref[...]
    vone_ref[:, HEAD_DIM:2 * HEAD_DIM] = jnp.ones((seq, HEAD_DIM), BF16)

    def reset_stats():
        mx_ref[...] = jnp.full(mx_ref.shape, -jnp.inf, F32)
        acc_ref[...] = jnp.zeros(acc_ref.shape, F32)

    def load_queries(j):
        q = q_ref[pl.ds(pl.multiple_of(j * TQ, TQ), TQ), :]
        lane = lax.broadcasted_iota(jnp.int32, (TQ, HEAD_DIM), 1)
        zero = jnp.zeros_like(q)
        scale = jnp.asarray(QK_DIM ** -0.5, BF16)
        qq_ref[0:TQ, :] = jnp.where(lane < QK_DIM, q, zero) * scale
        qq_ref[TQ:2 * TQ, :] = jnp.where(lane >= QK_DIM, q, zero) * scale

    def score_chunk(j, c, s_ref):
        k0 = pl.multiple_of(c * TK, TK)
        s = lax.dot_general(qq_ref[...], k_ref[pl.ds(k0, TK), :], NT_DIMS,
                            preferred_element_type=F32)
        ub = (seq - TQ) // LANES - j * (TQ // LANES) + c * kslabs
        bias = jnp.concatenate([bias_ref[ub + r] for r in range(kslabs)], axis=1)
        s = s + jnp.concatenate([bias, bias], axis=0)
        s_ref[c] = s
        mx = mx_ref[...]
        for r in range(kslabs):
            mx = jnp.maximum(mx, s[:, r * LANES:(r + 1) * LANES])
        mx_ref[...] = mx

    def prob_chunk(c, s_ref):
        k0 = pl.multiple_of(c * TK, TK)
        s = s_ref[c]
        m = mrep_ref[...]
        p = jnp.exp2(s - jnp.concatenate([m] * kslabs, axis=1))
        return jnp.dot(p.astype(BF16), vone_ref[pl.ds(k0, TK), :],
                       preferred_element_type=F32)

    def publish_max():
        m = jnp.max(mx_ref[...], axis=-1, keepdims=True)
        mrep_ref[...] = jnp.broadcast_to(m, mrep_ref.shape)

    def finalize(j):
        lq = lq_ref[...]
        lam = (jnp.exp(jnp.sum(lq[0:1] * lq[1:2], axis=-1, keepdims=True))
               - jnp.exp(jnp.sum(lq[2:3] * lq[3:4], axis=-1, keepdims=True)) + lam_init)
        o = acc_ref[:, 0:HEAD_DIM] / acc_ref[:, HEAD_DIM:2 * HEAD_DIM]
        o = o[0:TQ] - lam * o[TQ:2 * TQ]
        ms = jnp.mean(o * o, axis=-1, keepdims=True)
        o = o * lax.rsqrt(ms + EPS) * sg_ref[...] * (1.0 - lam_init)
        rows = pl.ds(pl.multiple_of(j * TQ, TQ), TQ)
        g = g_ref[rows, :].astype(F32)
        o_ref[rows, :] = (o * (g * jax.nn.sigmoid(g))).astype(BF16)

    def run_group(fn, chunk_tuples):
        total = None
        for cs in chunk_tuples:
            pv = fn(*cs)
            if pv is not None:
                total = pv if total is None else total + pv
        if total is not None:
            acc_ref[...] += total

    def for_chunks(band, fn, *blocks):
        if band:
            starts = [jnp.clip(jnp.asarray(j, jnp.int32) - BAND_SIDE, 0, nk - BAND_CHUNKS)
                      for j in blocks]
            run_group(fn, [[st + w for st in starts] for w in range(BAND_CHUNKS)])
        else:
            def group(gi, carry):
                run_group(fn, [[gi * group_chunks + w] * len(blocks)
                               for w in range(group_chunks)])
                return carry

            lax.fori_loop(0, nk // group_chunks, group, 0)

    def run(band):
        reset_stats()
        load_queries(0)
        for_chunks(band, lambda c: score_chunk(0, c, s_refs[0]), 0)
        publish_max()
        reset_stats()

        def block_step(j, parity):
            load_queries(j)
            cur, prev = s_refs[parity], s_refs[1 - parity]

            def both(cj, cp):
                score_chunk(j, cj, cur)
                return prob_chunk(cp, prev)

            if band:
                for_chunks(band, both, j, j - 1)
            else:
                def group(gi, carry):
                    run_group(both, [[gi * group_chunks + w] * 2 for w in range(group_chunks)])
                    return carry

                if nk > group_chunks:
                    lax.fori_loop(0, band_ref[N_HEADS] * (nk // group_chunks - 1), group, 0)
                run_group(both, [[c, c] for c in range(nk - group_chunks, nk)])
            finalize(j - 1)
            publish_max()
            reset_stats()

        def block_pair(t, carry):
            block_step(2 * t + 1, 1)
            block_step(2 * t + 2, 0)
            return carry

        lax.fori_loop(0, (nq - 1) // 2, block_pair, 0)
        if (nq - 1) % 2:
            block_step(nq - 1, (nq - 1) % 2)

        last = s_refs[(nq - 1) % 2]
        for_chunks(band, lambda c: prob_chunk(c, last), nq - 1)
        finalize(nq - 1)

    if nk <= BAND_CHUNKS:
        run(False)
        return

    def band_is_exact():
        sel = (lax.broadcasted_iota(jnp.int32, (HEAD_DIM, 2 * HEAD_DIM), 0) // QK_DIM
               == lax.broadcasted_iota(jnp.int32, (HEAD_DIM, 2 * HEAD_DIM), 1) // HEAD_DIM
               ).astype(BF16)

        def max_half_norm2(ref):
            def body(r, best):
                x = ref[pl.ds(pl.multiple_of(r * TK, TK), TK), :]
                n2 = jnp.dot(x * x, sel, preferred_element_type=F32)
                return jnp.maximum(best, jnp.max(n2.reshape(TK // 8, 8, 2 * HEAD_DIM), axis=0))

            best = lax.fori_loop(0, seq // TK, body, jnp.zeros((8, 2 * HEAD_DIM), F32),
                                 unroll=True)
            return jnp.max(best, axis=0, keepdims=True)

        n2 = max_half_norm2(q_ref) * max_half_norm2(k_ref) * NORM_SLACK
        slope = jnp.full((1, 2 * HEAD_DIM), slope_ref[h], F32)
        gap = slope * (LOG2E * BAND_DIST) - BAND_MARGIN
        half = gap * (0.5 * QK_DIM ** 0.5)
        violated = jnp.where((n2 <= half * half) & (gap > 0.0), 0, 1)
        return jnp.max(violated)

    violated = lax.cond(band_ref[h] > 0, band_is_exact, lambda: jnp.int32(1))
    lax.cond(violated == 0, lambda: run(True), lambda: run(False))


def _attention(z, lq, sg, *, batch, seq, layer):
    n = batch * seq
    slopes = jnp.asarray(ALIBI_SLOPES, F32)
    band_heads = jnp.asarray(
        [int(sl * LOG2E * BAND_DIST > BAND_MARGIN) for sl in ALIBI_SLOPES] + [1], jnp.int32)
    kernel = functools.partial(_attn_kernel, seq=seq, lam_init=_lambda_init(layer))
    n_bias = (2 * seq - TQ) // LANES
    col = lambda off: (lambda h, b: (b, off + h))
    return pl.pallas_call(
        kernel,
        grid=(N_HEADS, batch),
        in_specs=[
            pl.BlockSpec(memory_space=pltpu.SMEM),
            pl.BlockSpec(memory_space=pltpu.SMEM),
            pl.BlockSpec((seq, HEAD_DIM), col(0)),
            pl.BlockSpec((seq, HEAD_DIM), col(N_HEADS)),
            pl.BlockSpec((seq, HEAD_DIM), col(2 * N_HEADS)),
            pl.BlockSpec((seq, HEAD_DIM), col(3 * N_HEADS)),
            pl.BlockSpec((4, QK_DIM), lambda h, b: (0, 0)),
            pl.BlockSpec((1, HEAD_DIM), lambda h, b: (0, 0)),
        ],
        out_specs=pl.BlockSpec((seq, HEAD_DIM), col(0)),
        out_shape=jax.ShapeDtypeStruct((n, D_ATT), BF16),
        scratch_shapes=[
            pltpu.VMEM((2 * TQ, HEAD_DIM), BF16),
            pltpu.VMEM((n_bias, TQ, LANES), F32),
            pltpu.VMEM((seq, 2 * HEAD_DIM), BF16),
            pltpu.VMEM((seq // TK, 2 * TQ, TK), F32),
            pltpu.VMEM((seq // TK, 2 * TQ, TK), F32),
            pltpu.VMEM((2 * TQ, LANES), F32),
            pltpu.VMEM((2 * TQ, LANES), F32),
            pltpu.VMEM((2 * TQ, 2 * HEAD_DIM), F32),
        ],
        compiler_params=pltpu.CompilerParams(
            dimension_semantics=("parallel", "arbitrary"),
            vmem_limit_bytes=VMEM_LIMIT),
        name="diff_attn",
    )(slopes, band_heads, z, z, z, z, lq, sg)


def _out_rows(x_ref, att_ref, u_ref, vg_ref, gg_ref, vng_ref, ws_ref, bs_ref, wo_ref, sgu_ref):
    vg = vg_ref[...].astype(F32)
    ms = jnp.mean(vg * vg, axis=-1, keepdims=True)
    vn = (vg * lax.rsqrt(ms + EPS) * vng_ref[...]).astype(BF16)
    gg = gg_ref[...].astype(F32)
    gate = u_ref[...].astype(F32) * (gg * jax.nn.sigmoid(gg))
    for c in range(TM_PROJ // CHUNK):
        r0 = c * CHUNK
        for g in range(N_GROUPS):
            c0 = g * GROUP_DIM
            sv = jnp.dot(ws_ref[g], vn[r0:r0 + CHUNK, c0:c0 + GROUP_DIM],
                         preferred_element_type=F32) + bs_ref[g]
            sgu_ref[r0:r0 + CHUNK, c0:c0 + GROUP_DIM] = (
                gate[r0:r0 + CHUNK, c0:c0 + GROUP_DIM] * sv).astype(BF16)
    y = jnp.dot(att_ref[...], wo_ref[0:D_ATT, :], preferred_element_type=F32)
    y = y + jnp.dot(sgu_ref[...], wo_ref[D_ATT:D_MODEL, :], preferred_element_type=F32)
    return x_ref[...] + y


def _out_final_kernel(x_ref, att_ref, u_ref, vg_ref, gg_ref, vng_ref, ws_ref, bs_ref,
                      wo_ref, fg_ref, o_ref, sgu_ref):
    o = _out_rows(x_ref, att_ref, u_ref, vg_ref, gg_ref, vng_ref, ws_ref, bs_ref, wo_ref,
                  sgu_ref)
    ms = jnp.mean(o * o, axis=-1, keepdims=True)
    o_ref[...] = o * lax.rsqrt(ms + EPS) * fg_ref[...]


def _out_in_kernel(x_ref, att_ref, u_ref, vg_ref, gg_ref, vng_ref, ws_ref, bs_ref,
                   wo_ref, g_next_ref, w_next_ref, o_ref, z_ref, sgu_ref):
    o = _out_rows(x_ref, att_ref, u_ref, vg_ref, gg_ref, vng_ref, ws_ref, bs_ref, wo_ref,
                  sgu_ref)
    o_ref[...] = o
    _in_proj_rows(o, g_next_ref, w_next_ref, z_ref)


def _out_proj(x2, att, z, vng, ws, bs, wo, g, w_next=None):
    n = x2.shape[0]
    u_blk = 4 * D_ATT // D_GMLP
    rows = lambda width, blk: pl.BlockSpec((TM_PROJ, width), lambda r: (r, blk))
    whole = lambda shape: pl.BlockSpec(shape, lambda r: (0,) * len(shape))
    in_specs = [
        rows(D_MODEL, 0), rows(D_ATT, 0),
        rows(D_GMLP, u_blk), rows(D_GMLP, u_blk + 1), rows(D_GMLP, u_blk + 2),
        whole((1, D_GMLP)), whole((N_GROUPS, CHUNK, CHUNK)), whole((N_GROUPS, CHUNK, 1)),
        whole((D_MODEL, D_MODEL)), whole((1, D_MODEL)),
    ]
    operands = [x2, att, z, z, z, vng, ws, bs, wo, g]
    out_specs = rows(D_MODEL, 0)
    out_shape = jax.ShapeDtypeStruct((n, D_MODEL), F32)
    body = _out_final_kernel
    if w_next is not None:
        in_specs.append(whole((D_MODEL, D_IN)))
        operands.append(w_next)
        out_specs = (out_specs, rows(D_IN, 0))
        out_shape = (out_shape, jax.ShapeDtypeStruct((n, D_IN), BF16))
        body = _out_in_kernel
    return pl.pallas_call(
        body,
        grid=(n // TM_PROJ,),
        in_specs=in_specs,
        out_specs=out_specs,
        out_shape=out_shape,
        scratch_shapes=[pltpu.VMEM((TM_PROJ, D_GMLP), BF16)],
        compiler_params=pltpu.CompilerParams(
            dimension_semantics=("parallel",), vmem_limit_bytes=VMEM_LIMIT),
        name="out_proj" if w_next is None else "out_in_proj",
    )(*operands)


def _trunk(x, norm_g, w_in, lambda_qk, subln_g, vnorm_g, w_s, b_s, w_out, final_g):
    batch, seq, _ = x.shape
    x2 = x.reshape(batch * seq, D_MODEL)
    z = _in_proj(x2, norm_g[0].reshape(1, D_MODEL), w_in[0])
    for l in range(DEPTH):
        att = _attention(z, lambda_qk[l], subln_g[l].reshape(1, HEAD_DIM),
                         batch=batch, seq=seq, layer=l)
        layer_args = (x2, att, z, vnorm_g[l].reshape(1, D_GMLP), w_s[l],
                      b_s[l].reshape(N_GROUPS, CHUNK, 1), w_out[l])
        if l + 1 < DEPTH:
            x2, z = _out_proj(*layer_args, norm_g[l + 1].reshape(1, D_MODEL), w_in[l + 1])
        else:
            x2 = _out_proj(*layer_args, final_g.reshape(1, D_MODEL))
    return x2.reshape(batch, seq, D_MODEL)


def kernel(x_prompt, x_sample, norm_g, w_in, lambda_qk, subln_g, vnorm_g, w_s, b_s, w_out, final_g):
    w_in_b = w_in.astype(BF16)
    w_s_b = w_s.astype(BF16)
    w_out_b = w_out.astype(BF16)
    args = (norm_g, w_in_b, lambda_qk, subln_g, vnorm_g, w_s_b, b_s, w_out_b, final_g)
    return (_trunk(x_prompt, *args), _trunk(x_sample, *args))
```

```python
import functools
import math

import jax
import jax.numpy as jnp
from jax import lax
from jax.experimental import pallas as pl
from jax.experimental.pallas import tpu as pltpu

D_MODEL = 1024
DEPTH = 4
D_ATT = 512
N_HEADS = 4
HEAD_DIM = 128
QK_DIM = 64
D_GMLP = 512
N_GROUPS = 4
GROUP_DIM = 128
CHUNK = 128
D_IN = 4 * D_ATT + 3 * D_GMLP
EPS = 1e-6
LOG2E = math.log2(math.e)

LANES = 128
VMEM_LIMIT = 56 * 1024 * 1024

TM_PROJ = 512
TQ = 256
TK = 256
GROUP_CHUNKS = 16
BAND_SIDE = 3
BAND_CHUNKS = 2 * BAND_SIDE + 1
BAND_DIST = BAND_SIDE * TK + 1
BAND_MARGIN = 160.0
NORM_SLACK = 1.01
ALIBI_SLOPES = tuple(2.0 ** (-8.0 * (i + 1) / N_HEADS) for i in range(N_HEADS))
assert TK == TQ

F32 = jnp.float32
BF16 = jnp.bfloat16
NT_DIMS = (((1,), (1,)), ((), ()))


def _lambda_init(layer):
    return 0.8 - 0.6 * math.exp(-0.3 * layer)


def _in_proj_kernel(x_ref, g_ref, w_ref, z_ref):
    _in_proj_rows(x_ref[...], g_ref, w_ref, z_ref)


def _in_proj_rows(x, g_ref, w_ref, z_ref):
    ms = jnp.mean(x * x, axis=-1, keepdims=True)
    h = (x * lax.rsqrt(ms + EPS) * g_ref[...]).astype(BF16)
    for n0 in range(0, D_IN, D_ATT):
        zc = jnp.dot(h, w_ref[:, n0:n0 + D_ATT], preferred_element_type=F32)
        if n0 == D_ATT:
            zc = zc * LOG2E
        z_ref[:, n0:n0 + D_ATT] = zc.astype(BF16)


def _in_proj(x2, g, w):
    n = x2.shape[0]
    return pl.pallas_call(
        _in_proj_kernel,
        grid=(n // TM_PROJ,),
        in_specs=[
            pl.BlockSpec((TM_PROJ, D_MODEL), lambda r: (r, 0)),
            pl.BlockSpec((1, D_MODEL), lambda r: (0, 0)),
            pl.BlockSpec((D_MODEL, D_IN), lambda r: (0, 0)),
        ],
        out_specs=pl.BlockSpec((TM_PROJ, D_IN), lambda r: (r, 0)),
        out_shape=jax.ShapeDtypeStruct((n, D_IN), BF16),
        compiler_params=pltpu.CompilerParams(
            dimension_semantics=("parallel",), vmem_limit_bytes=VMEM_LIMIT),
        name="in_proj",
    )(x2, g, w)


def _attn_kernel(slope_ref, band_ref, q_ref, k_ref, v_ref, g_ref, lq_ref, sg_ref, o_ref,
                 qq_ref, bias_ref, vone_ref, s0_ref, s1_ref, mx_ref, mrep_ref, acc_ref,
                 *, seq, lam_init):
    s_refs = (s0_ref, s1_ref)
    h = pl.program_id(0)
    nq = seq // TQ
    nk = seq // TK
    group_chunks = max(d for d in range(1, GROUP_CHUNKS + 1) if nk % d == 0)
    kslabs = TK // LANES
    n_bias = (2 * seq - TQ) // LANES

    @pl.when(pl.program_id(1) == 0)
    def _():
        slope = slope_ref[h] * LOG2E
        t = lax.broadcasted_iota(jnp.int32, (TQ, LANES), 0) + (seq - TQ)
        x = lax.broadcasted_iota(jnp.int32, (TQ, LANES), 1)
        base = (t - x).astype(F32)

        def fill(u, carry):
            d = base - jnp.asarray(u * LANES, dtype=F32)
            bias_ref[u] = -slope * jnp.abs(d)
            return carry

        lax.fori_loop(0, n_bias, fill, 0)

    vone_ref[:, 0:HEAD_DIM] = v_ref[...]
    vone_ref[:, HEAD_DIM:2 * HEAD_DIM] = jnp.ones((seq, HEAD_DIM), BF16)

    def reset_stats():
        mx_ref[...] = jnp.full(mx_ref.shape, -jnp.inf, F32)
        acc_ref[...] = jnp.zeros(acc_ref.shape, F32)

    def load_queries(j):
        q = q_ref[pl.ds(pl.multiple_of(j * TQ, TQ), TQ), :]
        lane = lax.broadcasted_iota(jnp.int32, (TQ, HEAD_DIM), 1)
        zero = jnp.zeros_like(q)
        scale = jnp.asarray(QK_DIM ** -0.5, BF16)
        qq_ref[0:TQ, :] = jnp.where(lane < QK_DIM, q, zero) * scale
        qq_ref[TQ:2 * TQ, :] = jnp.where(lane >= QK_DIM, q, zero) * scale

    def score_chunk(j, c, s_ref):
        k0 = pl.multiple_of(c * TK, TK)
        s = lax.dot_general(qq_ref[...], k_ref[pl.ds(k0, TK), :], NT_DIMS,
                            preferred_element_type=F32)
        ub = (seq - TQ) // LANES - j * (TQ // LANES) + c * kslabs
        bias = jnp.concatenate([bias_ref[ub + r] for r in range(kslabs)], axis=1)
        s = s + jnp.concatenate([bias, bias], axis=0)
        s_ref[c] = s
        mx = mx_ref[...]
        for r in range(kslabs):
            mx = jnp.maximum(mx, s[:, r * LANES:(r + 1) * LANES])
        mx_ref[...] = mx

    def prob_chunk(c, s_ref):
        k0 = pl.multiple_of(c * TK, TK)
        s = s_ref[c]
        m = mrep_ref[...]
        p = jnp.exp2(s - jnp.concatenate([m] * kslabs, axis=1))
        return jnp.dot(p.astype(BF16), vone_ref[pl.ds(k0, TK), :],
                       preferred_element_type=F32)

    def publish_max():
        m = jnp.max(mx_ref[...], axis=-1, keepdims=True)
        mrep_ref[...] = jnp.broadcast_to(m, mrep_ref.shape)

    def finalize(j):
        lq = lq_ref[...]
        lam = (jnp.exp(jnp.sum(lq[0:1] * lq[1:2], axis=-1, keepdims=True))
               - jnp.exp(jnp.sum(lq[2:3] * lq[3:4], axis=-1, keepdims=True)) + lam_init)
        o = acc_ref[:, 0:HEAD_DIM] / acc_ref[:, HEAD_DIM:2 * HEAD_DIM]
        o = o[0:TQ] - lam * o[TQ:2 * TQ]
        ms = jnp.mean(o * o, axis=-1, keepdims=True)
        o = o * lax.rsqrt(ms + EPS) * sg_ref[...] * (1.0 - lam_init)
        rows = pl.ds(pl.multiple_of(j * TQ, TQ), TQ)
        g = g_ref[rows, :].astype(F32)
        o_ref[rows, :] = (o * (g * jax.nn.sigmoid(g))).astype(BF16)

    def run_group(fn, chunk_tuples):
        total = None
        for cs in chunk_tuples:
            pv = fn(*cs)
            if pv is not None:
                total = pv if total is None else total + pv
        if total is not None:
            acc_ref[...] += total

    def for_chunks(band, fn, *blocks):
        if band:
            starts = [jnp.clip(jnp.asarray(j, jnp.int32) - BAND_SIDE, 0, nk - BAND_CHUNKS)
                      for j in blocks]
            run_group(fn, [[st + w for st in starts] for w in range(BAND_CHUNKS)])
        else:
            def group(gi, carry):
                run_group(fn, [[gi * group_chunks + w] * len(blocks)
                               for w in range(group_chunks)])
                return carry

            lax.fori_loop(0, nk // group_chunks, group, 0)

    def run(band):
        reset_stats()
        load_queries(0)
        for_chunks(band, lambda c: score_chunk(0, c, s_refs[0]), 0)
        publish_max()
        reset_stats()

        def block_step(j, parity):
            load_queries(j)
            cur, prev = s_refs[parity], s_refs[1 - parity]

            def both(cj, cp):
                score_chunk(j, cj, cur)
                return prob_chunk(cp, prev)

            if band:
                for_chunks(band, both, j, j - 1)
            else:
                def group(gi, carry):
                    run_group(both, [[gi * group_chunks + w] * 2 for w in range(group_chunks)])
                    return carry

                if nk > group_chunks:
                    lax.fori_loop(0, band_ref[N_HEADS] * (nk // group_chunks - 1), group, 0)
                run_group(both, [[c, c] for c in range(nk - group_chunks, nk)])
            finalize(j - 1)
            publish_max()
            reset_stats()

        def block_pair(t, carry):
            block_step(2 * t + 1, 1)
            block_step(2 * t + 2, 0)
            return carry

        lax.fori_loop(0, (nq - 1) // 2, block_pair, 0)
        if (nq - 1) % 2:
            block_step(nq - 1, (nq - 1) % 2)

        last = s_refs[(nq - 1) % 2]
        for_chunks(band, lambda c: prob_chunk(c, last), nq - 1)
        finalize(nq - 1)

    if nk <= BAND_CHUNKS:
        run(False)
        return

    def band_is_exact():
        sel = (lax.broadcasted_iota(jnp.int32, (HEAD_DIM, 2 * HEAD_DIM), 0) // QK_DIM
               == lax.broadcasted_iota(jnp.int32, (HEAD_DIM, 2 * HEAD_DIM), 1) // HEAD_DIM
               ).astype(BF16)

        def max_half_norm2(ref):
            def body(r, best):
                x = ref[pl.ds(pl.multiple_of(r * TK, TK), TK), :]
                n2 = jnp.dot(x * x, sel, preferred_element_type=F32)
                return jnp.maximum(best, jnp.max(n2.reshape(TK // 8, 8, 2 * HEAD_DIM), axis=0))

            best = lax.fori_loop(0, seq // TK, body, jnp.zeros((8, 2 * HEAD_DIM), F32),
                                 unroll=True)
            return jnp.max(best, axis=0, keepdims=True)

        n2 = max_half_norm2(q_ref) * max_half_norm2(k_ref) * NORM_SLACK
        slope = jnp.full((1, 2 * HEAD_DIM), slope_ref[h], F32)
        gap = slope * (LOG2E * BAND_DIST) - BAND_MARGIN
        half = gap * (0.5 * QK_DIM ** 0.5)
        violated = jnp.where((n2 <= half * half) & (gap > 0.0), 0, 1)
        return jnp.max(violated)

    violated = lax.cond(band_ref[h] > 0, band_is_exact, lambda: jnp.int32(1))
    lax.cond(violated == 0, lambda: run(True), lambda: run(False))


def _attention(z, lq, sg, *, batch, seq, layer):
    n = batch * seq
    slopes = jnp.asarray(ALIBI_SLOPES, F32)
    band_heads = jnp.asarray(
        [int(sl * LOG2E * BAND_DIST > BAND_MARGIN) for sl in ALIBI_SLOPES] + [1], jnp.int32)
    kernel = functools.partial(_attn_kernel, seq=seq, lam_init=_lambda_init(layer))
    n_bias = (2 * seq - TQ) // LANES
    col = lambda off: (lambda h, b: (b, off + h))
    return pl.pallas_call(
        kernel,
        grid=(N_HEADS, batch),
        in_specs=[
            pl.BlockSpec(memory_space=pltpu.SMEM),
            pl.BlockSpec(memory_space=pltpu.SMEM),
            pl.BlockSpec((seq, HEAD_DIM), col(0)),
            pl.BlockSpec((seq, HEAD_DIM), col(N_HEADS)),
            pl.BlockSpec((seq, HEAD_DIM), col(2 * N_HEADS)),
            pl.BlockSpec((seq, HEAD_DIM), col(3 * N_HEADS)),
            pl.BlockSpec((4, QK_DIM), lambda h, b: (0, 0)),
            pl.BlockSpec((1, HEAD_DIM), lambda h, b: (0, 0)),
        ],
        out_specs=pl.BlockSpec((seq, HEAD_DIM), col(0)),
        out_shape=jax.ShapeDtypeStruct((n, D_ATT), BF16),
        scratch_shapes=[
            pltpu.VMEM((2 * TQ, HEAD_DIM), BF16),
            pltpu.VMEM((n_bias, TQ, LANES), F32),
            pltpu.VMEM((seq, 2 * HEAD_DIM), BF16),
            pltpu.VMEM((seq // TK, 2 * TQ, TK), F32),
            pltpu.VMEM((seq // TK, 2 * TQ, TK), F32),
            pltpu.VMEM((2 * TQ, LANES), F32),
            pltpu.VMEM((2 * TQ, LANES), F32),
            pltpu.VMEM((2 * TQ, 2 * HEAD_DIM), F32),
        ],
        compiler_params=pltpu.CompilerParams(
            dimension_semantics=("parallel", "arbitrary"),
            vmem_limit_bytes=VMEM_LIMIT),
        name="diff_attn",
    )(slopes, band_heads, z, z, z, z, lq, sg)


def _out_rows(x_ref, att_ref, u_ref, vg_ref, gg_ref, vng_ref, ws_ref, bs_ref, wo_ref, sgu_ref):
    vg = vg_ref[...].astype(F32)
    ms = jnp.mean(vg * vg, axis=-1, keepdims=True)
    vn = (vg * lax.rsqrt(ms + EPS) * vng_ref[...]).astype(BF16)
    gg = gg_ref[...].astype(F32)
    gate = u_ref[...].astype(F32) * (gg * jax.nn.sigmoid(gg))
    n_chunks = TM_PROJ // CHUNK
    for g in range(N_GROUPS):
        c0 = g * GROUP_DIM
        vn_g = jnp.concatenate(
            [vn[c * CHUNK:(c + 1) * CHUNK, c0:c0 + GROUP_DIM] for c in range(n_chunks)], axis=1)
        sv = jnp.dot(ws_ref[g], vn_g, preferred_element_type=F32) + bs_ref[g]
        for c in range(n_chunks):
            r0 = c * CHUNK
            sgu_ref[r0:r0 + CHUNK, c0:c0 + GROUP_DIM] = (
                gate[r0:r0 + CHUNK, c0:c0 + GROUP_DIM]
                * sv[:, c * GROUP_DIM:(c + 1) * GROUP_DIM]).astype(BF16)
    y = jnp.dot(att_ref[...], wo_ref[0:D_ATT, :], preferred_element_type=F32)
    y = y + jnp.dot(sgu_ref[...], wo_ref[D_ATT:D_MODEL, :], preferred_element_type=F32)
    return x_ref[...] + y


def _out_final_kernel(x_ref, att_ref, u_ref, vg_ref, gg_ref, vng_ref, ws_ref, bs_ref,
                      wo_ref, fg_ref, o_ref, sgu_ref):
    o = _out_rows(x_ref, att_ref, u_ref, vg_ref, gg_ref, vng_ref, ws_ref, bs_ref, wo_ref,
                  sgu_ref)
    ms = jnp.mean(o * o, axis=-1, keepdims=True)
    o_ref[...] = o * lax.rsqrt(ms + EPS) * fg_ref[...]


def _out_in_kernel(x_ref, att_ref, u_ref, vg_ref, gg_ref, vng_ref, ws_ref, bs_ref,
                   wo_ref, g_next_ref, w_next_ref, o_ref, z_ref, sgu_ref):
    o = _out_rows(x_ref, att_ref, u_ref, vg_ref, gg_ref, vng_ref, ws_ref, bs_ref, wo_ref,
                  sgu_ref)
    o_ref[...] = o
    _in_proj_rows(o, g_next_ref, w_next_ref, z_ref)


def _out_proj(x2, att, z, vng, ws, bs, wo, g, w_next=None):
    n = x2.shape[0]
    u_blk = 4 * D_ATT // D_GMLP
    rows = lambda width, blk: pl.BlockSpec((TM_PROJ, width), lambda r: (r, blk))
    whole = lambda shape: pl.BlockSpec(shape, lambda r: (0,) * len(shape))
    in_specs = [
        rows(D_MODEL, 0), rows(D_ATT, 0),
        rows(D_GMLP, u_blk), rows(D_GMLP, u_blk + 1), rows(D_GMLP, u_blk + 2),
        whole((1, D_GMLP)), whole((N_GROUPS, CHUNK, CHUNK)), whole((N_GROUPS, CHUNK, 1)),
        whole((D_MODEL, D_MODEL)), whole((1, D_MODEL)),
    ]
    operands = [x2, att, z, z, z, vng, ws, bs, wo, g]
    out_specs = rows(D_MODEL, 0)
    out_shape = jax.ShapeDtypeStruct((n, D_MODEL), F32)
    body = _out_final_kernel
    if w_next is not None:
        in_specs.append(whole((D_MODEL, D_IN)))
        operands.append(w_next)
        out_specs = (out_specs, rows(D_IN, 0))
        out_shape = (out_shape, jax.ShapeDtypeStruct((n, D_IN), BF16))
        body = _out_in_kernel
    return pl.pallas_call(
        body,
        grid=(n // TM_PROJ,),
        in_specs=in_specs,
        out_specs=out_specs,
        out_shape=out_shape,
        scratch_shapes=[pltpu.VMEM((TM_PROJ, D_GMLP), BF16)],
        compiler_params=pltpu.CompilerParams(
            dimension_semantics=("parallel",), vmem_limit_bytes=VMEM_LIMIT),
        name="out_proj" if w_next is None else "out_in_proj",
    )(*operands)


def _trunk(x, norm_g, w_in, lambda_qk, subln_g, vnorm_g, w_s, b_s, w_out, final_g):
    batch, seq, _ = x.shape
    x2 = x.reshape(batch * seq, D_MODEL)
    z = _in_proj(x2, norm_g[0].reshape(1, D_MODEL), w_in[0])
    for l in range(DEPTH):
        att = _attention(z, lambda_qk[l], subln_g[l].reshape(1, HEAD_DIM),
                         batch=batch, seq=seq, layer=l)
        layer_args = (x2, att, z, vnorm_g[l].reshape(1, D_GMLP), w_s[l],
                      b_s[l].reshape(N_GROUPS, CHUNK, 1), w_out[l])
        if l + 1 < DEPTH:
            x2, z = _out_proj(*layer_args, norm_g[l + 1].reshape(1, D_MODEL), w_in[l + 1])
        else:
            x2 = _out_proj(*layer_args, final_g.reshape(1, D_MODEL))
    return x2.reshape(batch, seq, D_MODEL)


def kernel(x_prompt, x_sample, norm_g, w_in, lambda_qk, subln_g, vnorm_g, w_s, b_s, w_out, final_g):
    w_in_b = w_in.astype(BF16)
    w_s_b = w_s.astype(BF16)
    w_out_b = w_out.astype(BF16)
    args = (norm_g, w_in_b, lambda_qk, subln_g, vnorm_g, w_s_b, b_s, w_out_b, final_g)
    return (_trunk(x_prompt, *args), _trunk(x_sample, *args))
```

```python
import functools
import math

import jax
import jax.numpy as jnp
from jax import lax
from jax.experimental import pallas as pl
from jax.experimental.pallas import tpu as pltpu

D_MODEL = 1024
DEPTH = 4
D_ATT = 512
N_HEADS = 4
HEAD_DIM = 128
QK_DIM = 64
D_GMLP = 512
N_GROUPS = 4
GROUP_DIM = 128
CHUNK = 128
D_IN = 4 * D_ATT + 3 * D_GMLP
EPS = 1e-6
LOG2E = math.log2(math.e)

LANES = 128
SUBLANES = 8
VMEM_LIMIT = 56 * 1024 * 1024

TM_PROJ = 512
TQ = 256
TK = 256
GROUP_CHUNKS = 16
STEPS_PER_TRIP = 4
BAND_SIDE = 3
BAND_CHUNKS = 2 * BAND_SIDE + 1
BAND_DIST = BAND_SIDE * TK + 1
BAND_MARGIN = 160.0
NORM_SLACK = 1.01
ALIBI_SLOPES = tuple(2.0 ** (-8.0 * (i + 1) / N_HEADS) for i in range(N_HEADS))
assert TK == TQ

F32 = jnp.float32
BF16 = jnp.bfloat16
NT_DIMS = (((1,), (1,)), ((), ()))


def _lambda_init(layer):
    return 0.8 - 0.6 * math.exp(-0.3 * layer)


def _in_proj_kernel(x_ref, g_ref, w_ref, z_ref):
    _in_proj_rows(x_ref[...], g_ref, w_ref, z_ref)


def _in_proj_rows(x, g_ref, w_ref, z_ref):
    ms = jnp.mean(x * x, axis=-1, keepdims=True)
    h = (x * lax.rsqrt(ms + EPS) * g_ref[...]).astype(BF16)
    for n0 in range(0, D_IN, D_ATT):
        zc = jnp.dot(h, w_ref[:, n0:n0 + D_ATT], preferred_element_type=F32)
        if n0 == D_ATT:
            zc = zc * LOG2E
        z_ref[:, n0:n0 + D_ATT] = zc.astype(BF16)


def _in_proj(x2, g, w):
    n = x2.shape[0]
    return pl.pallas_call(
        _in_proj_kernel,
        grid=(n // TM_PROJ,),
        in_specs=[
            pl.BlockSpec((TM_PROJ, D_MODEL), lambda r: (r, 0)),
            pl.BlockSpec((1, D_MODEL), lambda r: (0, 0)),
            pl.BlockSpec((D_MODEL, D_IN), lambda r: (0, 0)),
        ],
        out_specs=pl.BlockSpec((TM_PROJ, D_IN), lambda r: (r, 0)),
        out_shape=jax.ShapeDtypeStruct((n, D_IN), BF16),
        compiler_params=pltpu.CompilerParams(
            dimension_semantics=("parallel",), vmem_limit_bytes=VMEM_LIMIT),
        name="in_proj",
    )(x2, g, w)


def _attn_kernel(slope_ref, band_ref, q_ref, k_ref, v_ref, g_ref, lq_ref, sg_ref, o_ref,
                 qq_ref, bias_ref, vone_ref, s0_ref, s1_ref, mx_ref, mrep_ref, acc_ref,
                 *, seq, lam_init):
    s_refs = (s0_ref, s1_ref)
    h = pl.program_id(0)
    nq = seq // TQ
    nk = seq // TK
    group_chunks = max(d for d in range(1, GROUP_CHUNKS + 1) if nk % d == 0)
    kslabs = TK // LANES
    n_bias = (2 * seq - TQ) // LANES

    @pl.when(pl.program_id(1) == 0)
    def _():
        slope = slope_ref[h] * LOG2E
        t = lax.broadcasted_iota(jnp.int32, (TQ, LANES), 0) + (seq - TQ)
        x = lax.broadcasted_iota(jnp.int32, (TQ, LANES), 1)
        base = (t - x).astype(F32)

        def fill(u, carry):
            d = base - jnp.asarray(u * LANES, dtype=F32)
            bias_ref[u] = -slope * jnp.abs(d)
            return carry

        lax.fori_loop(0, n_bias, fill, 0)

    vone_ref[:, 0:HEAD_DIM] = v_ref[...]
    vone_ref[:, HEAD_DIM:2 * HEAD_DIM] = jnp.ones((seq, HEAD_DIM), BF16)

    def reset_stats():
        mx_ref[...] = jnp.full(mx_ref.shape, -jnp.inf, F32)
        acc_ref[...] = jnp.zeros(acc_ref.shape, F32)

    def load_queries(j):
        q = q_ref[pl.ds(pl.multiple_of(j * TQ, TQ), TQ), :]
        lane = lax.broadcasted_iota(jnp.int32, (TQ, HEAD_DIM), 1)
        zero = jnp.zeros_like(q)
        scale = jnp.asarray(QK_DIM ** -0.5, BF16)
        qq_ref[0:TQ, :] = jnp.where(lane < QK_DIM, q, zero) * scale
        qq_ref[TQ:2 * TQ, :] = jnp.where(lane >= QK_DIM, q, zero) * scale

    def score_chunk(j, c, s_ref):
        k0 = pl.multiple_of(c * TK, TK)
        s = lax.dot_general(qq_ref[...], k_ref[pl.ds(k0, TK), :], NT_DIMS,
                            preferred_element_type=F32)
        ub = (seq - TQ) // LANES - j * (TQ // LANES) + c * kslabs
        bias = jnp.concatenate([bias_ref[ub + r] for r in range(kslabs)], axis=1)
        s = s + jnp.concatenate([bias, bias], axis=0)
        s_ref[c] = s
        mx = mx_ref[...]
        for r in range(kslabs):
            mx = jnp.maximum(mx, s[:, r * LANES:(r + 1) * LANES])
        mx_ref[...] = mx

    def prob_chunk(c, s_ref):
        k0 = pl.multiple_of(c * TK, TK)
        s = s_ref[c]
        m = mrep_ref[...]
        p = jnp.exp2(s - jnp.concatenate([m] * kslabs, axis=1))
        return jnp.dot(p.astype(BF16), vone_ref[pl.ds(k0, TK), :],
                       preferred_element_type=F32)

    def publish_max():
        m = jnp.max(mx_ref[...], axis=-1, keepdims=True)
        mrep_ref[...] = jnp.broadcast_to(m, mrep_ref.shape)

    def finalize(j):
        lq = lq_ref[...]
        lam = (jnp.exp(jnp.sum(lq[0:1] * lq[1:2], axis=-1, keepdims=True))
               - jnp.exp(jnp.sum(lq[2:3] * lq[3:4], axis=-1, keepdims=True)) + lam_init)
        o = acc_ref[:, 0:HEAD_DIM] / acc_ref[:, HEAD_DIM:2 * HEAD_DIM]
        o = o[0:TQ] - lam * o[TQ:2 * TQ]
        ms = jnp.mean(o * o, axis=-1, keepdims=True)
        o = o * lax.rsqrt(ms + EPS) * sg_ref[...] * (1.0 - lam_init)
        rows = pl.ds(pl.multiple_of(j * TQ, TQ), TQ)
        g = g_ref[rows, :].astype(F32)
        o_ref[rows, :] = (o * (g * jax.nn.sigmoid(g))).astype(BF16)

    def run_group(fn, chunk_tuples):
        total = None
        for cs in chunk_tuples:
            pv = fn(*cs)
            if pv is not None:
                total = pv if total is None else total + pv
        if total is not None:
            acc_ref[...] += total

    def for_chunks(band, fn, *blocks):
        if band:
            starts = [jnp.clip(jnp.asarray(j, jnp.int32) - BAND_SIDE, 0, nk - BAND_CHUNKS)
                      for j in blocks]
            run_group(fn, [[st + w for st in starts] for w in range(BAND_CHUNKS)])
        else:
            def group(gi, carry):
                run_group(fn, [[gi * group_chunks + w] * len(blocks)
                               for w in range(group_chunks)])
                return carry

            lax.fori_loop(0, nk // group_chunks, group, 0)

    def run(band):
        reset_stats()
        load_queries(0)
        for_chunks(band, lambda c: score_chunk(0, c, s_refs[0]), 0)
        publish_max()
        reset_stats()

        def block_step(j, parity):
            load_queries(j)
            cur, prev = s_refs[parity], s_refs[1 - parity]

            def both(cj, cp):
                score_chunk(j, cj, cur)
                return prob_chunk(cp, prev)

            if band:
                for_chunks(band, both, j, j - 1)
            else:
                def group(gi, carry):
                    run_group(both, [[gi * group_chunks + w] * 2 for w in range(group_chunks)])
                    return carry

                if nk > group_chunks:
                    lax.fori_loop(0, band_ref[N_HEADS] * (nk // group_chunks - 1), group, 0)
                run_group(both, [[c, c] for c in range(nk - group_chunks, nk)])
            finalize(j - 1)
            publish_max()
            reset_stats()

        def block_steps(t, carry):
            for i in range(STEPS_PER_TRIP):
                block_step(STEPS_PER_TRIP * t + 1 + i, (1 + i) % 2)
            return carry

        trips = (nq - 1) // STEPS_PER_TRIP
        lax.fori_loop(0, trips, block_steps, 0)
        for j in range(STEPS_PER_TRIP * trips + 1, nq):
            block_step(j, j % 2)

        last = s_refs[(nq - 1) % 2]
        for_chunks(band, lambda c: prob_chunk(c, last), nq - 1)
        finalize(nq - 1)

    if nk <= BAND_CHUNKS:
        run(False)
        return

    def band_is_exact():
        sel = (lax.broadcasted_iota(jnp.int32, (HEAD_DIM, 2 * HEAD_DIM), 0) // QK_DIM
               == lax.broadcasted_iota(jnp.int32, (HEAD_DIM, 2 * HEAD_DIM), 1) // HEAD_DIM
               ).astype(BF16)

        def max_half_norm2(ref):
            def body(r, best):
                x = ref[pl.ds(pl.multiple_of(r * TK, TK), TK), :]
                n2 = jnp.dot(x * x, sel, preferred_element_type=F32)
                return jnp.maximum(
                    best, jnp.max(n2.reshape(TK // SUBLANES, SUBLANES, 2 * HEAD_DIM), axis=0))

            best = lax.fori_loop(0, seq // TK, body, jnp.zeros((SUBLANES, 2 * HEAD_DIM), F32),
                                 unroll=True)
            return jnp.max(best, axis=0, keepdims=True)

        n2 = max_half_norm2(q_ref) * max_half_norm2(k_ref) * NORM_SLACK
        slope = jnp.full((1, 2 * HEAD_DIM), slope_ref[h], F32)
        gap = slope * (LOG2E * BAND_DIST) - BAND_MARGIN
        half = gap * (0.5 * QK_DIM ** 0.5)
        violated = jnp.where((n2 <= half * half) & (gap > 0.0), 0, 1)
        return jnp.max(violated)

    violated = lax.cond(band_ref[h] > 0, band_is_exact, lambda: jnp.int32(1))
    lax.cond(violated == 0, lambda: run(True), lambda: run(False))


def _attention(z, lq, sg, *, batch, seq, layer):
    n = batch * seq
    slopes = jnp.asarray(ALIBI_SLOPES, F32)
    band_heads = jnp.asarray(
        [int(sl * LOG2E * BAND_DIST > BAND_MARGIN) for sl in ALIBI_SLOPES] + [1], jnp.int32)
    kernel = functools.partial(_attn_kernel, seq=seq, lam_init=_lambda_init(layer))
    n_bias = (2 * seq - TQ) // LANES
    col = lambda off: (lambda h, b: (b, off + h))
    return pl.pallas_call(
        kernel,
        grid=(N_HEADS, batch),
        in_specs=[
            pl.BlockSpec(memory_space=pltpu.SMEM),
            pl.BlockSpec(memory_space=pltpu.SMEM),
            pl.BlockSpec((seq, HEAD_DIM), col(0)),
            pl.BlockSpec((seq, HEAD_DIM), col(N_HEADS)),
            pl.BlockSpec((seq, HEAD_DIM), col(2 * N_HEADS)),
            pl.BlockSpec((seq, HEAD_DIM), col(3 * N_HEADS)),
            pl.BlockSpec((4, QK_DIM), lambda h, b: (0, 0)),
            pl.BlockSpec((1, HEAD_DIM), lambda h, b: (0, 0)),
        ],
        out_specs=pl.BlockSpec((seq, HEAD_DIM), col(0)),
        out_shape=jax.ShapeDtypeStruct((n, D_ATT), BF16),
        scratch_shapes=[
            pltpu.VMEM((2 * TQ, HEAD_DIM), BF16),
            pltpu.VMEM((n_bias, TQ, LANES), F32),
            pltpu.VMEM((seq, 2 * HEAD_DIM), BF16),
            pltpu.VMEM((seq // TK, 2 * TQ, TK), F32),
            pltpu.VMEM((seq // TK, 2 * TQ, TK), F32),
            pltpu.VMEM((2 * TQ, LANES), F32),
            pltpu.VMEM((2 * TQ, LANES), F32),
            pltpu.VMEM((2 * TQ, 2 * HEAD_DIM), F32),
        ],
        compiler_params=pltpu.CompilerParams(
            dimension_semantics=("parallel", "arbitrary"),
            vmem_limit_bytes=VMEM_LIMIT),
        name="diff_attn",
    )(slopes, band_heads, z, z, z, z, lq, sg)


def _out_rows(x_ref, att_ref, u_ref, vg_ref, gg_ref, vng_ref, ws_ref, bs_ref, wo_ref, sgu_ref):
    vg = vg_ref[...].astype(F32)
    ms = jnp.mean(vg * vg, axis=-1, keepdims=True)
    vn = (vg * lax.rsqrt(ms + EPS) * vng_ref[...]).astype(BF16)
    gg = gg_ref[...].astype(F32)
    gate = u_ref[...].astype(F32) * (gg * jax.nn.sigmoid(gg))
    n_chunks = TM_PROJ // CHUNK
    for g in range(N_GROUPS):
        c0 = g * GROUP_DIM
        vn_g = jnp.concatenate(
            [vn[c * CHUNK:(c + 1) * CHUNK, c0:c0 + GROUP_DIM] for c in range(n_chunks)], axis=1)
        sv = jnp.dot(ws_ref[g], vn_g, preferred_element_type=F32) + bs_ref[g]
        for c in range(n_chunks):
            r0 = c * CHUNK
            sgu_ref[r0:r0 + CHUNK, c0:c0 + GROUP_DIM] = (
                gate[r0:r0 + CHUNK, c0:c0 + GROUP_DIM]
                * sv[:, c * GROUP_DIM:(c + 1) * GROUP_DIM]).astype(BF16)
    y = jnp.dot(att_ref[...], wo_ref[0:D_ATT, :], preferred_element_type=F32)
    y = y + jnp.dot(sgu_ref[...], wo_ref[D_ATT:D_MODEL, :], preferred_element_type=F32)
    return x_ref[...] + y


def _out_final_kernel(x_ref, att_ref, u_ref, vg_ref, gg_ref, vng_ref, ws_ref, bs_ref,
                      wo_ref, fg_ref, o_ref, sgu_ref):
    o = _out_rows(x_ref, att_ref, u_ref, vg_ref, gg_ref, vng_ref, ws_ref, bs_ref, wo_ref,
                  sgu_ref)
    ms = jnp.mean(o * o, axis=-1, keepdims=True)
    o_ref[...] = o * lax.rsqrt(ms + EPS) * fg_ref[...]


def _out_in_kernel(x_ref, att_ref, u_ref, vg_ref, gg_ref, vng_ref, ws_ref, bs_ref,
                   wo_ref, g_next_ref, w_next_ref, o_ref, z_ref, sgu_ref):
    o = _out_rows(x_ref, att_ref, u_ref, vg_ref, gg_ref, vng_ref, ws_ref, bs_ref, wo_ref,
                  sgu_ref)
    o_ref[...] = o
    _in_proj_rows(o, g_next_ref, w_next_ref, z_ref)


def _out_proj(x2, att, z, vng, ws, bs, wo, g, w_next=None):
    n = x2.shape[0]
    u_blk = 4 * D_ATT // D_GMLP
    rows = lambda width, blk: pl.BlockSpec((TM_PROJ, width), lambda r: (r, blk))
    whole = lambda shape: pl.BlockSpec(shape, lambda r: (0,) * len(shape))
    in_specs = [
        rows(D_MODEL, 0), rows(D_ATT, 0),
        rows(D_GMLP, u_blk), rows(D_GMLP, u_blk + 1), rows(D_GMLP, u_blk + 2),
        whole((1, D_GMLP)), whole((N_GROUPS, CHUNK, CHUNK)), whole((N_GROUPS, CHUNK, 1)),
        whole((D_MODEL, D_MODEL)), whole((1, D_MODEL)),
    ]
    operands = [x2, att, z, z, z, vng, ws, bs, wo, g]
    out_specs = rows(D_MODEL, 0)
    out_shape = jax.ShapeDtypeStruct((n, D_MODEL), F32)
    body = _out_final_kernel
    if w_next is not None:
        in_specs.append(whole((D_MODEL, D_IN)))
        operands.append(w_next)
        out_specs = (out_specs, rows(D_IN, 0))
        out_shape = (out_shape, jax.ShapeDtypeStruct((n, D_IN), BF16))
        body = _out_in_kernel
    return pl.pallas_call(
        body,
        grid=(n // TM_PROJ,),
        in_specs=in_specs,
        out_specs=out_specs,
        out_shape=out_shape,
        scratch_shapes=[pltpu.VMEM((TM_PROJ, D_GMLP), BF16)],
        compiler_params=pltpu.CompilerParams(
            dimension_semantics=("parallel",), vmem_limit_bytes=VMEM_LIMIT),
        name="out_proj" if w_next is None else "out_in_proj",
    )(*operands)


def _trunk(x, norm_g, w_in, lambda_qk, subln_g, vnorm_g, w_s, b_s, w_out, final_g):
    batch, seq, _ = x.shape
    x2 = x.reshape(batch * seq, D_MODEL)
    z = _in_proj(x2, norm_g[0].reshape(1, D_MODEL), w_in[0])
    for l in range(DEPTH):
        att = _attention(z, lambda_qk[l], subln_g[l].reshape(1, HEAD_DIM),
                         batch=batch, seq=seq, layer=l)
        layer_args = (x2, att, z, vnorm_g[l].reshape(1, D_GMLP), w_s[l],
                      b_s[l].reshape(N_GROUPS, CHUNK, 1), w_out[l])
        if l + 1 < DEPTH:
            x2, z = _out_proj(*layer_args, norm_g[l + 1].reshape(1, D_MODEL), w_in[l + 1])
        else:
            x2 = _out_proj(*layer_args, final_g.reshape(1, D_MODEL))
    return x2.reshape(batch, seq, D_MODEL)


def kernel(x_prompt, x_sample, norm_g, w_in, lambda_qk, subln_g, vnorm_g, w_s, b_s, w_out, final_g):
    w_in_b = w_in.astype(BF16)
    w_s_b = w_s.astype(BF16)
    w_out_b = w_out.astype(BF16)
    args = (norm_g, w_in_b, lambda_qk, subln_g, vnorm_g, w_s_b, b_s, w_out_b, final_g)
    return (_trunk(x_prompt, *args), _trunk(x_sample, *args))
```

```python
import functools
import math

import jax
import jax.numpy as jnp
from jax import lax
from jax.experimental import pallas as pl
from jax.experimental.pallas import tpu as pltpu

D_MODEL = 1024
DEPTH = 4
D_ATT = 512
N_HEADS = 4
HEAD_DIM = 128
QK_DIM = 64
D_GMLP = 512
N_GROUPS = 4
GROUP_DIM = 128
CHUNK = 128
D_IN = 4 * D_ATT + 3 * D_GMLP
EPS = 1e-6
LOG2E = math.log2(math.e)

LANES = 128
SUBLANES = 8
VMEM_LIMIT = 56 * 1024 * 1024

TM_PROJ = 512
PROJ_SPLIT = 2
PROJ_ROWS = TM_PROJ * PROJ_SPLIT
TQ = 256
TK = 256
GROUP_CHUNKS = 16
STEPS_PER_TRIP = 4
BAND_SIDE = 3
BAND_CHUNKS = 2 * BAND_SIDE + 1
BAND_DIST = BAND_SIDE * TK + 1
BAND_MARGIN = 160.0
NORM_SLACK = 1.01
ALIBI_SLOPES = tuple(2.0 ** (-8.0 * (i + 1) / N_HEADS) for i in range(N_HEADS))
assert TK == TQ

F32 = jnp.float32
BF16 = jnp.bfloat16
NT_DIMS = (((1,), (1,)), ((), ()))


def _lambda_init(layer):
    return 0.8 - 0.6 * math.exp(-0.3 * layer)


def _sub_rows(ref, i):
    return ref.at[pl.ds(i * TM_PROJ, TM_PROJ)]


def _in_proj_kernel(x_ref, g_ref, w_ref, z_ref):
    for i in range(PROJ_SPLIT):
        _in_proj_rows(_sub_rows(x_ref, i)[...], g_ref, w_ref, _sub_rows(z_ref, i))


def _in_proj_rows(x, g_ref, w_ref, z_ref):
    ms = jnp.mean(x * x, axis=-1, keepdims=True)
    h = (x * lax.rsqrt(ms + EPS) * g_ref[...]).astype(BF16)
    for n0 in range(0, D_IN, D_ATT):
        zc = jnp.dot(h, w_ref[:, n0:n0 + D_ATT], preferred_element_type=F32)
        if n0 == D_ATT:
            zc = zc * LOG2E
        z_ref[:, n0:n0 + D_ATT] = zc.astype(BF16)


def _whole_spec(shape):
    return pl.BlockSpec(shape, lambda r: (0,) * len(shape), pipeline_mode=pl.Buffered(1))


def _in_proj(x2, g, w):
    n = x2.shape[0]
    return pl.pallas_call(
        _in_proj_kernel,
        grid=(n // PROJ_ROWS,),
        in_specs=[
            pl.BlockSpec((PROJ_ROWS, D_MODEL), lambda r: (r, 0)),
            _whole_spec((1, D_MODEL)),
            _whole_spec((D_MODEL, D_IN)),
        ],
        out_specs=pl.BlockSpec((PROJ_ROWS, D_IN), lambda r: (r, 0)),
        out_shape=jax.ShapeDtypeStruct((n, D_IN), BF16),
        compiler_params=pltpu.CompilerParams(
            dimension_semantics=("parallel",), vmem_limit_bytes=VMEM_LIMIT),
        name="in_proj",
    )(x2, g, w)


def _attn_kernel(slope_ref, band_ref, q_ref, k_ref, v_ref, g_ref, lq_ref, sg_ref, o_ref,
                 qq_ref, bias_ref, vone_ref, s0_ref, s1_ref, mx_ref, mrep_ref, acc_ref,
                 *, seq, lam_init):
    s_refs = (s0_ref, s1_ref)
    h = pl.program_id(0)
    nq = seq // TQ
    nk = seq // TK
    group_chunks = max(d for d in range(1, GROUP_CHUNKS + 1) if nk % d == 0)
    kslabs = TK // LANES
    n_bias = (2 * seq - TQ) // LANES

    @pl.when(pl.program_id(1) == 0)
    def _():
        slope = slope_ref[h] * LOG2E
        t = lax.broadcasted_iota(jnp.int32, (TQ, LANES), 0) + (seq - TQ)
        x = lax.broadcasted_iota(jnp.int32, (TQ, LANES), 1)
        base = (t - x).astype(F32)

        def fill(u, carry):
            d = base - jnp.asarray(u * LANES, dtype=F32)
            bias_ref[u] = -slope * jnp.abs(d)
            return carry

        lax.fori_loop(0, n_bias, fill, 0)

    vone_ref[:, 0:HEAD_DIM] = v_ref[...]
    vone_ref[:, HEAD_DIM:2 * HEAD_DIM] = jnp.ones((seq, HEAD_DIM), BF16)

    def reset_stats():
        mx_ref[...] = jnp.full(mx_ref.shape, -jnp.inf, F32)
        acc_ref[...] = jnp.zeros(acc_ref.shape, F32)

    def load_queries(j):
        q = q_ref[pl.ds(pl.multiple_of(j * TQ, TQ), TQ), :]
        lane = lax.broadcasted_iota(jnp.int32, (TQ, HEAD_DIM), 1)
        zero = jnp.zeros_like(q)
        scale = jnp.asarray(QK_DIM ** -0.5, BF16)
        qq_ref[0:TQ, :] = jnp.where(lane < QK_DIM, q, zero) * scale
        qq_ref[TQ:2 * TQ, :] = jnp.where(lane >= QK_DIM, q, zero) * scale

    def score_chunk(j, c, s_ref):
        k0 = pl.multiple_of(c * TK, TK)
        s = lax.dot_general(qq_ref[...], k_ref[pl.ds(k0, TK), :], NT_DIMS,
                            preferred_element_type=F32)
        ub = (seq - TQ) // LANES - j * (TQ // LANES) + c * kslabs
        bias = jnp.concatenate([bias_ref[ub + r] for r in range(kslabs)], axis=1)
        s = s + jnp.concatenate([bias, bias], axis=0)
        s_ref[c] = s
        mx = mx_ref[...]
        for r in range(kslabs):
            mx = jnp.maximum(mx, s[:, r * LANES:(r + 1) * LANES])
        mx_ref[...] = mx

    def prob_chunk(c, s_ref):
        k0 = pl.multiple_of(c * TK, TK)
        s = s_ref[c]
        m = mrep_ref[...]
        p = jnp.exp2(s - jnp.concatenate([m] * kslabs, axis=1))
        return jnp.dot(p.astype(BF16), vone_ref[pl.ds(k0, TK), :],
                       preferred_element_type=F32)

    def publish_max():
        m = jnp.max(mx_ref[...], axis=-1, keepdims=True)
        mrep_ref[...] = jnp.broadcast_to(m, mrep_ref.shape)

    def finalize(j):
        lq = lq_ref[...]
        lam = (jnp.exp(jnp.sum(lq[0:1] * lq[1:2], axis=-1, keepdims=True))
               - jnp.exp(jnp.sum(lq[2:3] * lq[3:4], axis=-1, keepdims=True)) + lam_init)
        o = acc_ref[:, 0:HEAD_DIM] / acc_ref[:, HEAD_DIM:2 * HEAD_DIM]
        o = o[0:TQ] - lam * o[TQ:2 * TQ]
        ms = jnp.mean(o * o, axis=-1, keepdims=True)
        o = o * lax.rsqrt(ms + EPS) * sg_ref[...] * (1.0 - lam_init)
        rows = pl.ds(pl.multiple_of(j * TQ, TQ), TQ)
        g = g_ref[rows, :].astype(F32)
        o_ref[rows, :] = (o * (g * jax.nn.sigmoid(g))).astype(BF16)

    def run_group(fn, chunk_tuples):
        total = None
        for cs in chunk_tuples:
            pv = fn(*cs)
            if pv is not None:
                total = pv if total is None else total + pv
        if total is not None:
            acc_ref[...] += total

    def for_chunks(band, fn, *blocks):
        if band:
            starts = [jnp.clip(jnp.asarray(j, jnp.int32) - BAND_SIDE, 0, nk - BAND_CHUNKS)
                      for j in blocks]
            run_group(fn, [[st + w for st in starts] for w in range(BAND_CHUNKS)])
        else:
            def group(gi, carry):
                run_group(fn, [[gi * group_chunks + w] * len(blocks)
                               for w in range(group_chunks)])
                return carry

            lax.fori_loop(0, nk // group_chunks, group, 0)

    def run(band):
        reset_stats()
        load_queries(0)
        for_chunks(band, lambda c: score_chunk(0, c, s_refs[0]), 0)
        publish_max()
        reset_stats()

        def block_step(j, parity):
            load_queries(j)
            cur, prev = s_refs[parity], s_refs[1 - parity]

            def both(cj, cp):
                score_chunk(j, cj, cur)
                return prob_chunk(cp, prev)

            if band:
                for_chunks(band, both, j, j - 1)
            else:
                def group(gi, carry):
                    run_group(both, [[gi * group_chunks + w] * 2 for w in range(group_chunks)])
                    return carry

                if nk > group_chunks:
                    lax.fori_loop(0, band_ref[N_HEADS] * (nk // group_chunks - 1), group, 0)
                run_group(both, [[c, c] for c in range(nk - group_chunks, nk)])
            finalize(j - 1)
            publish_max()
            reset_stats()

        def block_steps(t, carry):
            for i in range(STEPS_PER_TRIP):
                block_step(STEPS_PER_TRIP * t + 1 + i, (1 + i) % 2)
            return carry

        trips = (nq - 1) // STEPS_PER_TRIP
        lax.fori_loop(0, trips, block_steps, 0)
        for j in range(STEPS_PER_TRIP * trips + 1, nq):
            block_step(j, j % 2)

        last = s_refs[(nq - 1) % 2]
        for_chunks(band, lambda c: prob_chunk(c, last), nq - 1)
        finalize(nq - 1)

    if nk <= BAND_CHUNKS:
        run(False)
        return

    def band_is_exact():
        sel = (lax.broadcasted_iota(jnp.int32, (HEAD_DIM, 2 * HEAD_DIM), 0) // QK_DIM
               == lax.broadcasted_iota(jnp.int32, (HEAD_DIM, 2 * HEAD_DIM), 1) // HEAD_DIM
               ).astype(BF16)

        def max_half_norm2(ref):
            def body(r, best):
                x = ref[pl.ds(pl.multiple_of(r * TK, TK), TK), :]
                n2 = jnp.dot(x * x, sel, preferred_element_type=F32)
                return jnp.maximum(
                    best, jnp.max(n2.reshape(TK // SUBLANES, SUBLANES, 2 * HEAD_DIM), axis=0))

            best = lax.fori_loop(0, seq // TK, body, jnp.zeros((SUBLANES, 2 * HEAD_DIM), F32),
                                 unroll=True)
            return jnp.max(best, axis=0, keepdims=True)

        n2 = max_half_norm2(q_ref) * max_half_norm2(k_ref) * NORM_SLACK
        slope = jnp.full((1, 2 * HEAD_DIM), slope_ref[h], F32)
        gap = slope * (LOG2E * BAND_DIST) - BAND_MARGIN
        half = gap * (0.5 * QK_DIM ** 0.5)
        violated = jnp.where((n2 <= half * half) & (gap > 0.0), 0, 1)
        return jnp.max(violated)

    violated = lax.cond(band_ref[h] > 0, band_is_exact, lambda: jnp.int32(1))
    lax.cond(violated == 0, lambda: run(True), lambda: run(False))


def _attention(z, lq, sg, *, batch, seq, layer):
    n = batch * seq
    slopes = jnp.asarray(ALIBI_SLOPES, F32)
    band_heads = jnp.asarray(
        [int(sl * LOG2E * BAND_DIST > BAND_MARGIN) for sl in ALIBI_SLOPES] + [1], jnp.int32)
    kernel = functools.partial(_attn_kernel, seq=seq, lam_init=_lambda_init(layer))
    n_bias = (2 * seq - TQ) // LANES
    col = lambda off: (lambda h, b: (b, off + h))
    return pl.pallas_call(
        kernel,
        grid=(N_HEADS, batch),
        in_specs=[
            pl.BlockSpec(memory_space=pltpu.SMEM),
            pl.BlockSpec(memory_space=pltpu.SMEM),
            pl.BlockSpec((seq, HEAD_DIM), col(0)),
            pl.BlockSpec((seq, HEAD_DIM), col(N_HEADS)),
            pl.BlockSpec((seq, HEAD_DIM), col(2 * N_HEADS)),
            pl.BlockSpec((seq, HEAD_DIM), col(3 * N_HEADS)),
            pl.BlockSpec((4, QK_DIM), lambda h, b: (0, 0)),
            pl.BlockSpec((1, HEAD_DIM), lambda h, b: (0, 0)),
        ],
        out_specs=pl.BlockSpec((seq, HEAD_DIM), col(0)),
        out_shape=jax.ShapeDtypeStruct((n, D_ATT), BF16),
        scratch_shapes=[
            pltpu.VMEM((2 * TQ, HEAD_DIM), BF16),
            pltpu.VMEM((n_bias, TQ, LANES), F32),
            pltpu.VMEM((seq, 2 * HEAD_DIM), BF16),
            pltpu.VMEM((seq // TK, 2 * TQ, TK), F32),
            pltpu.VMEM((seq // TK, 2 * TQ, TK), F32),
            pltpu.VMEM((2 * TQ, LANES), F32),
            pltpu.VMEM((2 * TQ, LANES), F32),
            pltpu.VMEM((2 * TQ, 2 * HEAD_DIM), F32),
        ],
        compiler_params=pltpu.CompilerParams(
            dimension_semantics=("parallel", "arbitrary"),
            vmem_limit_bytes=VMEM_LIMIT),
        name="diff_attn",
    )(slopes, band_heads, z, z, z, z, lq, sg)


def _out_rows(x_ref, att_ref, u_ref, vg_ref, gg_ref, vng_ref, ws_ref, bs_ref, wo_ref, sgu_ref):
    vg = vg_ref[...].astype(F32)
    ms = jnp.mean(vg * vg, axis=-1, keepdims=True)
    vn = (vg * lax.rsqrt(ms + EPS) * vng_ref[...]).astype(BF16)
    gg = gg_ref[...].astype(F32)
    gate = u_ref[...].astype(F32) * (gg * jax.nn.sigmoid(gg))
    n_chunks = TM_PROJ // CHUNK
    for g in range(N_GROUPS):
        c0 = g * GROUP_DIM
        vn_g = jnp.concatenate(
            [vn[c * CHUNK:(c + 1) * CHUNK, c0:c0 + GROUP_DIM] for c in range(n_chunks)], axis=1)
        sv = jnp.dot(ws_ref[g], vn_g, preferred_element_type=F32) + bs_ref[g]
        for c in range(n_chunks):
            r0 = c * CHUNK
            sgu_ref[r0:r0 + CHUNK, c0:c0 + GROUP_DIM] = (
                gate[r0:r0 + CHUNK, c0:c0 + GROUP_DIM]
                * sv[:, c * GROUP_DIM:(c + 1) * GROUP_DIM]).astype(BF16)
    y = jnp.dot(att_ref[...], wo_ref[0:D_ATT, :], preferred_element_type=F32)
    y = y + jnp.dot(sgu_ref[...], wo_ref[D_ATT:D_MODEL, :], preferred_element_type=F32)
    return x_ref[...] + y


def _out_final_kernel(x_ref, att_ref, u_ref, vg_ref, gg_ref, vng_ref, ws_ref, bs_ref,
                      wo_ref, fg_ref, o_ref, sgu_ref):
    for i in range(PROJ_SPLIT):
        o = _out_rows(*[_sub_rows(r, i) for r in (x_ref, att_ref, u_ref, vg_ref, gg_ref)],
                      vng_ref, ws_ref, bs_ref, wo_ref, sgu_ref.at[i])
        ms = jnp.mean(o * o, axis=-1, keepdims=True)
        _sub_rows(o_ref, i)[...] = o * lax.rsqrt(ms + EPS) * fg_ref[...]


def _out_in_kernel(x_ref, att_ref, u_ref, vg_ref, gg_ref, vng_ref, ws_ref, bs_ref,
                   wo_ref, g_next_ref, w_next_ref, o_ref, z_ref, sgu_ref):
    for i in range(PROJ_SPLIT):
        o = _out_rows(*[_sub_rows(r, i) for r in (x_ref, att_ref, u_ref, vg_ref, gg_ref)],
                      vng_ref, ws_ref, bs_ref, wo_ref, sgu_ref.at[i])
        _sub_rows(o_ref, i)[...] = o
        _in_proj_rows(o, g_next_ref, w_next_ref, _sub_rows(z_ref, i))


def _out_proj(x2, att, z, vng, ws, bs, wo, g, w_next=None):
    n = x2.shape[0]
    u_blk = 4 * D_ATT // D_GMLP
    rows = lambda width, blk: pl.BlockSpec((PROJ_ROWS, width), lambda r: (r, blk))
    whole = _whole_spec
    in_specs = [
        rows(D_MODEL, 0), rows(D_ATT, 0),
        rows(D_GMLP, u_blk), rows(D_GMLP, u_blk + 1), rows(D_GMLP, u_blk + 2),
        whole((1, D_GMLP)), whole((N_GROUPS, CHUNK, CHUNK)), whole((N_GROUPS, CHUNK, 1)),
        whole((D_MODEL, D_MODEL)), whole((1, D_MODEL)),
    ]
    operands = [x2, att, z, z, z, vng, ws, bs, wo, g]
    out_specs = rows(D_MODEL, 0)
    out_shape = jax.ShapeDtypeStruct((n, D_MODEL), F32)
    body = _out_final_kernel
    if w_next is not None:
        in_specs.append(whole((D_MODEL, D_IN)))
        operands.append(w_next)
        out_specs = (out_specs, rows(D_IN, 0))
        out_shape = (out_shape, jax.ShapeDtypeStruct((n, D_IN), BF16))
        body = _out_in_kernel
    return pl.pallas_call(
        body,
        grid=(n // PROJ_ROWS,),
        in_specs=in_specs,
        out_specs=out_specs,
        out_shape=out_shape,
        scratch_shapes=[pltpu.VMEM((PROJ_SPLIT, TM_PROJ, D_GMLP), BF16)],
        compiler_params=pltpu.CompilerParams(
            dimension_semantics=("parallel",), vmem_limit_bytes=VMEM_LIMIT),
        name="out_proj" if w_next is None else "out_in_proj",
    )(*operands)


def _trunk(x, norm_g, w_in, lambda_qk, subln_g, vnorm_g, w_s, b_s, w_out, final_g):
    batch, seq, _ = x.shape
    x2 = x.reshape(batch * seq, D_MODEL)
    z = _in_proj(x2, norm_g[0].reshape(1, D_MODEL), w_in[0])
    for l in range(DEPTH):
        att = _attention(z, lambda_qk[l], subln_g[l].reshape(1, HEAD_DIM),
                         batch=batch, seq=seq, layer=l)
        layer_args = (x2, att, z, vnorm_g[l].reshape(1, D_GMLP), w_s[l],
                      b_s[l].reshape(N_GROUPS, CHUNK, 1), w_out[l])
        if l + 1 < DEPTH:
            x2, z = _out_proj(*layer_args, norm_g[l + 1].reshape(1, D_MODEL), w_in[l + 1])
        else:
            x2 = _out_proj(*layer_args, final_g.reshape(1, D_MODEL))
    return x2.reshape(batch, seq, D_MODEL)


def kernel(x_prompt, x_sample, norm_g, w_in, lambda_qk, subln_g, vnorm_g, w_s, b_s, w_out, final_g):
    w_in_b = w_in.astype(BF16)
    w_s_b = w_s.astype(BF16)
    w_out_b = w_out.astype(BF16)
    args = (norm_g, w_in_b, lambda_qk, subln_g, vnorm_g, w_s_b, b_s, w_out_b, final_g)
    return (_trunk(x_prompt, *args), _trunk(x_sample, *args))
```

```python
import functools
import math

import jax
import jax.numpy as jnp
from jax import lax
from jax.experimental import pallas as pl
from jax.experimental.pallas import tpu as pltpu

D_MODEL = 1024
DEPTH = 4
D_ATT = 512
N_HEADS = 4
HEAD_DIM = 128
QK_DIM = 64
D_GMLP = 512
N_GROUPS = 4
GROUP_DIM = 128
CHUNK = 128
D_IN = 4 * D_ATT + 3 * D_GMLP
EPS = 1e-6
LOG2E = math.log2(math.e)

LANES = 128
SUBLANES = 8
VMEM_LIMIT = 56 * 1024 * 1024

TM_PROJ = 512
PROJ_SPLIT = 2
PROJ_ROWS = TM_PROJ * PROJ_SPLIT
TQ = 256
TK = 256
GROUP_CHUNKS = 16
STEPS_PER_TRIP = 6
BAND_SIDE = 3
BAND_CHUNKS = 2 * BAND_SIDE + 1
BAND_DIST = BAND_SIDE * TK + 1
BAND_MARGIN = 160.0
NORM_SLACK = 1.01
ALIBI_SLOPES = tuple(2.0 ** (-8.0 * (i + 1) / N_HEADS) for i in range(N_HEADS))
assert TK == TQ

F32 = jnp.float32
BF16 = jnp.bfloat16
NT_DIMS = (((1,), (1,)), ((), ()))


def _lambda_init(layer):
    return 0.8 - 0.6 * math.exp(-0.3 * layer)


def _sub_rows(ref, i):
    return ref.at[pl.ds(i * TM_PROJ, TM_PROJ)]


def _in_proj_kernel(x_ref, g_ref, w_ref, z_ref):
    for i in range(PROJ_SPLIT):
        _in_proj_rows(_sub_rows(x_ref, i)[...], g_ref, w_ref, _sub_rows(z_ref, i))


def _in_proj_rows(x, g_ref, w_ref, z_ref):
    ms = jnp.mean(x * x, axis=-1, keepdims=True)
    h = (x * lax.rsqrt(ms + EPS) * g_ref[...]).astype(BF16)
    for n0 in range(0, D_IN, D_ATT):
        zc = jnp.dot(h, w_ref[:, n0:n0 + D_ATT], preferred_element_type=F32)
        if n0 == D_ATT:
            zc = zc * LOG2E
        z_ref[:, n0:n0 + D_ATT] = zc.astype(BF16)


def _whole_spec(shape):
    return pl.BlockSpec(shape, lambda r: (0,) * len(shape), pipeline_mode=pl.Buffered(1))


def _in_proj(x2, g, w):
    n = x2.shape[0]
    return pl.pallas_call(
        _in_proj_kernel,
        grid=(n // PROJ_ROWS,),
        in_specs=[
            pl.BlockSpec((PROJ_ROWS, D_MODEL), lambda r: (r, 0)),
            _whole_spec((1, D_MODEL)),
            _whole_spec((D_MODEL, D_IN)),
        ],
        out_specs=pl.BlockSpec((PROJ_ROWS, D_IN), lambda r: (r, 0)),
        out_shape=jax.ShapeDtypeStruct((n, D_IN), BF16),
        compiler_params=pltpu.CompilerParams(
            dimension_semantics=("parallel",), vmem_limit_bytes=VMEM_LIMIT),
        name="in_proj",
    )(x2, g, w)


def _attn_kernel(slope_ref, band_ref, q_ref, k_ref, v_ref, g_ref, lq_ref, sg_ref, o_ref,
                 qq_ref, bias_ref, vone_ref, s0_ref, s1_ref, mx_ref, mrep_ref, acc_ref,
                 *, seq, lam_init):
    s_refs = (s0_ref, s1_ref)
    h = pl.program_id(0)
    nq = seq // TQ
    nk = seq // TK
    group_chunks = max(d for d in range(1, GROUP_CHUNKS + 1) if nk % d == 0)
    kslabs = TK // LANES
    n_bias = (2 * seq - TQ) // LANES

    @pl.when(pl.program_id(1) == 0)
    def _():
        slope = slope_ref[h] * LOG2E
        t = lax.broadcasted_iota(jnp.int32, (TQ, LANES), 0) + (seq - TQ)
        x = lax.broadcasted_iota(jnp.int32, (TQ, LANES), 1)
        base = (t - x).astype(F32)

        def fill(u, carry):
            d = base - jnp.asarray(u * LANES, dtype=F32)
            bias_ref[u] = -slope * jnp.abs(d)
            return carry

        lax.fori_loop(0, n_bias, fill, 0)

    vone_ref[:, 0:HEAD_DIM] = v_ref[...]
    vone_ref[:, HEAD_DIM:2 * HEAD_DIM] = jnp.ones((seq, HEAD_DIM), BF16)

    def reset_stats():
        mx_ref[...] = jnp.full(mx_ref.shape, -jnp.inf, F32)
        acc_ref[...] = jnp.zeros(acc_ref.shape, F32)

    def load_queries(j):
        q = q_ref[pl.ds(pl.multiple_of(j * TQ, TQ), TQ), :]
        lane = lax.broadcasted_iota(jnp.int32, (TQ, HEAD_DIM), 1)
        zero = jnp.zeros_like(q)
        scale = jnp.asarray(QK_DIM ** -0.5, BF16)
        qq_ref[0:TQ, :] = jnp.where(lane < QK_DIM, q, zero) * scale
        qq_ref[TQ:2 * TQ, :] = jnp.where(lane >= QK_DIM, q, zero) * scale

    def score_chunk(j, c, s_ref):
        k0 = pl.multiple_of(c * TK, TK)
        s = lax.dot_general(qq_ref[...], k_ref[pl.ds(k0, TK), :], NT_DIMS,
                            preferred_element_type=F32)
        ub = (seq - TQ) // LANES - j * (TQ // LANES) + c * kslabs
        bias = jnp.concatenate([bias_ref[ub + r] for r in range(kslabs)], axis=1)
        s = s + jnp.concatenate([bias, bias], axis=0)
        s_ref[c] = s
        mx = mx_ref[...]
        for r in range(kslabs):
            mx = jnp.maximum(mx, s[:, r * LANES:(r + 1) * LANES])
        mx_ref[...] = mx

    def prob_chunk(c, s_ref):
        k0 = pl.multiple_of(c * TK, TK)
        s = s_ref[c]
        m = mrep_ref[...]
        p = jnp.exp2(s - jnp.concatenate([m] * kslabs, axis=1))
        return jnp.dot(p.astype(BF16), vone_ref[pl.ds(k0, TK), :],
                       preferred_element_type=F32)

    def publish_max():
        m = jnp.max(mx_ref[...], axis=-1, keepdims=True)
        mrep_ref[...] = jnp.broadcast_to(m, mrep_ref.shape)

    def finalize(j):
        lq = lq_ref[...]
        lam = (jnp.exp(jnp.sum(lq[0:1] * lq[1:2], axis=-1, keepdims=True))
               - jnp.exp(jnp.sum(lq[2:3] * lq[3:4], axis=-1, keepdims=True)) + lam_init)
        o = acc_ref[:, 0:HEAD_DIM] / acc_ref[:, HEAD_DIM:2 * HEAD_DIM]
        o = o[0:TQ] - lam * o[TQ:2 * TQ]
        ms = jnp.mean(o * o, axis=-1, keepdims=True)
        o = o * lax.rsqrt(ms + EPS) * sg_ref[...] * (1.0 - lam_init)
        rows = pl.ds(pl.multiple_of(j * TQ, TQ), TQ)
        g = g_ref[rows, :].astype(F32)
        o_ref[rows, :] = (o * (g * jax.nn.sigmoid(g))).astype(BF16)

    def run_group(fn, chunk_tuples):
        total = None
        for cs in chunk_tuples:
            pv = fn(*cs)
            if pv is not None:
                total = pv if total is None else total + pv
        if total is not None:
            acc_ref[...] += total

    def for_chunks(band, fn, *blocks):
        if band:
            starts = [jnp.clip(jnp.asarray(j, jnp.int32) - BAND_SIDE, 0, nk - BAND_CHUNKS)
                      for j in blocks]
            run_group(fn, [[st + w for st in starts] for w in range(BAND_CHUNKS)])
        else:
            def group(gi, carry):
                run_group(fn, [[gi * group_chunks + w] * len(blocks)
                               for w in range(group_chunks)])
                return carry

            lax.fori_loop(0, nk // group_chunks, group, 0)

    def run(band):
        reset_stats()
        load_queries(0)
        for_chunks(band, lambda c: score_chunk(0, c, s_refs[0]), 0)
        publish_max()
        reset_stats()

        def block_step(j, parity):
            load_queries(j)
            cur, prev = s_refs[parity], s_refs[1 - parity]

            def both(cj, cp):
                score_chunk(j, cj, cur)
                return prob_chunk(cp, prev)

            if band:
                for_chunks(band, both, j, j - 1)
            else:
                def group(gi, carry):
                    run_group(both, [[gi * group_chunks + w] * 2 for w in range(group_chunks)])
                    return carry

                if nk > group_chunks:
                    lax.fori_loop(0, band_ref[N_HEADS] * (nk // group_chunks - 1), group, 0)
                run_group(both, [[c, c] for c in range(nk - group_chunks, nk)])
            finalize(j - 1)
            publish_max()
            reset_stats()

        def block_steps(t, carry):
            for i in range(STEPS_PER_TRIP):
                block_step(STEPS_PER_TRIP * t + 1 + i, (1 + i) % 2)
            return carry

        trips = (nq - 1) // STEPS_PER_TRIP
        lax.fori_loop(0, trips, block_steps, 0)
        for j in range(STEPS_PER_TRIP * trips + 1, nq):
            block_step(j, j % 2)

        last = s_refs[(nq - 1) % 2]
        for_chunks(band, lambda c: prob_chunk(c, last), nq - 1)
        finalize(nq - 1)

    if nk <= BAND_CHUNKS:
        run(False)
        return

    def band_is_exact():
        sel = (lax.broadcasted_iota(jnp.int32, (HEAD_DIM, 2 * HEAD_DIM), 0) // QK_DIM
               == lax.broadcasted_iota(jnp.int32, (HEAD_DIM, 2 * HEAD_DIM), 1) // HEAD_DIM
               ).astype(BF16)

        def max_half_norm2(ref):
            def body(r, best):
                x = ref[pl.ds(pl.multiple_of(r * TK, TK), TK), :]
                n2 = jnp.dot(x * x, sel, preferred_element_type=F32)
                return jnp.maximum(
                    best, jnp.max(n2.reshape(TK // SUBLANES, SUBLANES, 2 * HEAD_DIM), axis=0))

            best = lax.fori_loop(0, seq // TK, body, jnp.zeros((SUBLANES, 2 * HEAD_DIM), F32),
                                 unroll=True)
            return jnp.max(best, axis=0, keepdims=True)

        n2 = max_half_norm2(q_ref) * max_half_norm2(k_ref) * NORM_SLACK
        slope = jnp.full((1, 2 * HEAD_DIM), slope_ref[h], F32)
        gap = slope * (LOG2E * BAND_DIST) - BAND_MARGIN
        half = gap * (0.5 * QK_DIM ** 0.5)
        violated = jnp.where((n2 <= half * half) & (gap > 0.0), 0, 1)
        return jnp.max(violated)

    violated = lax.cond(band_ref[h] > 0, band_is_exact, lambda: jnp.int32(1))
    lax.cond(violated == 0, lambda: run(True), lambda: run(False))


def _attention(z, lq, sg, *, batch, seq, layer):
    n = batch * seq
    slopes = jnp.asarray(ALIBI_SLOPES, F32)
    band_heads = jnp.asarray(
        [int(sl * LOG2E * BAND_DIST > BAND_MARGIN) for sl in ALIBI_SLOPES] + [1], jnp.int32)
    kernel = functools.partial(_attn_kernel, seq=seq, lam_init=_lambda_init(layer))
    n_bias = (2 * seq - TQ) // LANES
    col = lambda off: (lambda h, b: (b, off + h))
    return pl.pallas_call(
        kernel,
        grid=(N_HEADS, batch),
        in_specs=[
            pl.BlockSpec(memory_space=pltpu.SMEM),
            pl.BlockSpec(memory_space=pltpu.SMEM),
            pl.BlockSpec((seq, HEAD_DIM), col(0)),
            pl.BlockSpec((seq, HEAD_DIM), col(N_HEADS)),
            pl.BlockSpec((seq, HEAD_DIM), col(2 * N_HEADS)),
            pl.BlockSpec((seq, HEAD_DIM), col(3 * N_HEADS)),
            pl.BlockSpec((4, QK_DIM), lambda h, b: (0, 0)),
            pl.BlockSpec((1, HEAD_DIM), lambda h, b: (0, 0)),
        ],
        out_specs=pl.BlockSpec((seq, HEAD_DIM), col(0)),
        out_shape=jax.ShapeDtypeStruct((n, D_ATT), BF16),
        scratch_shapes=[
            pltpu.VMEM((2 * TQ, HEAD_DIM), BF16),
            pltpu.VMEM((n_bias, TQ, LANES), F32),
            pltpu.VMEM((seq, 2 * HEAD_DIM), BF16),
            pltpu.VMEM((seq // TK, 2 * TQ, TK), F32),
            pltpu.VMEM((seq // TK, 2 * TQ, TK), F32),
            pltpu.VMEM((2 * TQ, LANES), F32),
            pltpu.VMEM((2 * TQ, LANES), F32),
            pltpu.VMEM((2 * TQ, 2 * HEAD_DIM), F32),
        ],
        compiler_params=pltpu.CompilerParams(
            dimension_semantics=("parallel", "arbitrary"),
            vmem_limit_bytes=VMEM_LIMIT),
        name="diff_attn",
    )(slopes, band_heads, z, z, z, z, lq, sg)


def _out_rows(x_ref, att_ref, u_ref, vg_ref, gg_ref, vng_ref, ws_ref, bs_ref, wo_ref, sgu_ref):
    vg = vg_ref[...].astype(F32)
    ms = jnp.mean(vg * vg, axis=-1, keepdims=True)
    vn = (vg * lax.rsqrt(ms + EPS) * vng_ref[...]).astype(BF16)
    gg = gg_ref[...].astype(F32)
    gate = u_ref[...].astype(F32) * (gg * jax.nn.sigmoid(gg))
    n_chunks = TM_PROJ // CHUNK
    for g in range(N_GROUPS):
        c0 = g * GROUP_DIM
        vn_g = jnp.concatenate(
            [vn[c * CHUNK:(c + 1) * CHUNK, c0:c0 + GROUP_DIM] for c in range(n_chunks)], axis=1)
        sv = jnp.dot(ws_ref[g], vn_g, preferred_element_type=F32) + bs_ref[g]
        for c in range(n_chunks):
            r0 = c * CHUNK
            sgu_ref[r0:r0 + CHUNK, c0:c0 + GROUP_DIM] = (
                gate[r0:r0 + CHUNK, c0:c0 + GROUP_DIM]
                * sv[:, c * GROUP_DIM:(c + 1) * GROUP_DIM]).astype(BF16)
    y = jnp.dot(att_ref[...], wo_ref[0:D_ATT, :], preferred_element_type=F32)
    y = y + jnp.dot(sgu_ref[...], wo_ref[D_ATT:D_MODEL, :], preferred_element_type=F32)
    return x_ref[...] + y


def _out_final_kernel(x_ref, att_ref, u_ref, vg_ref, gg_ref, vng_ref, ws_ref, bs_ref,
                      wo_ref, fg_ref, o_ref, sgu_ref):
    for i in range(PROJ_SPLIT):
        o = _out_rows(*[_sub_rows(r, i) for r in (x_ref, att_ref, u_ref, vg_ref, gg_ref)],
                      vng_ref, ws_ref, bs_ref, wo_ref, sgu_ref.at[i])
        ms = jnp.mean(o * o, axis=-1, keepdims=True)
        _sub_rows(o_ref, i)[...] = o * lax.rsqrt(ms + EPS) * fg_ref[...]


def _out_in_kernel(x_ref, att_ref, u_ref, vg_ref, gg_ref, vng_ref, ws_ref, bs_ref,
                   wo_ref, g_next_ref, w_next_ref, o_ref, z_ref, sgu_ref):
    for i in range(PROJ_SPLIT):
        o = _out_rows(*[_sub_rows(r, i) for r in (x_ref, att_ref, u_ref, vg_ref, gg_ref)],
                      vng_ref, ws_ref, bs_ref, wo_ref, sgu_ref.at[i])
        _sub_rows(o_ref, i)[...] = o
        _in_proj_rows(o, g_next_ref, w_next_ref, _sub_rows(z_ref, i))


def _out_proj(x2, att, z, vng, ws, bs, wo, g, w_next=None):
    n = x2.shape[0]
    u_blk = 4 * D_ATT // D_GMLP
    rows = lambda width, blk: pl.BlockSpec((PROJ_ROWS, width), lambda r: (r, blk))
    whole = _whole_spec
    in_specs = [
        rows(D_MODEL, 0), rows(D_ATT, 0),
        rows(D_GMLP, u_blk), rows(D_GMLP, u_blk + 1), rows(D_GMLP, u_blk + 2),
        whole((1, D_GMLP)), whole((N_GROUPS, CHUNK, CHUNK)), whole((N_GROUPS, CHUNK, 1)),
        whole((D_MODEL, D_MODEL)), whole((1, D_MODEL)),
    ]
    operands = [x2, att, z, z, z, vng, ws, bs, wo, g]
    out_specs = rows(D_MODEL, 0)
    out_shape = jax.ShapeDtypeStruct((n, D_MODEL), F32)
    body = _out_final_kernel
    if w_next is not None:
        in_specs.append(whole((D_MODEL, D_IN)))
        operands.append(w_next)
        out_specs = (out_specs, rows(D_IN, 0))
        out_shape = (out_shape, jax.ShapeDtypeStruct((n, D_IN), BF16))
        body = _out_in_kernel
    return pl.pallas_call(
        body,
        grid=(n // PROJ_ROWS,),
        in_specs=in_specs,
        out_specs=out_specs,
        out_shape=out_shape,
        scratch_shapes=[pltpu.VMEM((PROJ_SPLIT, TM_PROJ, D_GMLP), BF16)],
        compiler_params=pltpu.CompilerParams(
            dimension_semantics=("parallel",), vmem_limit_bytes=VMEM_LIMIT),
        name="out_proj" if w_next is None else "out_in_proj",
    )(*operands)


def _trunk(x, norm_g, w_in, lambda_qk, subln_g, vnorm_g, w_s, b_s, w_out, final_g):
    batch, seq, _ = x.shape
    x2 = x.reshape(batch * seq, D_MODEL)
    z = _in_proj(x2, norm_g[0].reshape(1, D_MODEL), w_in[0])
    for l in range(DEPTH):
        att = _attention(z, lambda_qk[l], subln_g[l].reshape(1, HEAD_DIM),
                         batch=batch, seq=seq, layer=l)
        layer_args = (x2, att, z, vnorm_g[l].reshape(1, D_GMLP), w_s[l],
                      b_s[l].reshape(N_GROUPS, CHUNK, 1), w_out[l])
        if l + 1 < DEPTH:
            x2, z = _out_proj(*layer_args, norm_g[l + 1].reshape(1, D_MODEL), w_in[l + 1])
        else:
            x2 = _out_proj(*layer_args, final_g.reshape(1, D_MODEL))
    return x2.reshape(batch, seq, D_MODEL)


def kernel(x_prompt, x_sample, norm_g, w_in, lambda_qk, subln_g, vnorm_g, w_s, b_s, w_out, final_g):
    w_in_b = w_in.astype(BF16)
    w_s_b = w_s.astype(BF16)
    w_out_b = w_out.astype(BF16)
    args = (norm_g, w_in_b, lambda_qk, subln_g, vnorm_g, w_s_b, b_s, w_out_b, final_g)
    return (_trunk(x_prompt, *args), _trunk(x_sample, *args))
```

```python
import functools
import math

import jax
import jax.numpy as jnp
from jax import lax
from jax.experimental import pallas as pl
from jax.experimental.pallas import tpu as pltpu

D_MODEL = 1024
DEPTH = 4
D_ATT = 512
N_HEADS = 4
HEAD_DIM = 128
QK_DIM = 64
D_GMLP = 512
N_GROUPS = 4
GROUP_DIM = 128
CHUNK = 128
D_IN = 4 * D_ATT + 3 * D_GMLP
EPS = 1e-6
LOG2E = math.log2(math.e)

LANES = 128
SUBLANES = 8
VMEM_LIMIT = 56 * 1024 * 1024

TM_PROJ = 512
PROJ_SPLIT = 2
PROJ_ROWS = TM_PROJ * PROJ_SPLIT
TQ = 256
TK = 256
GROUP_CHUNKS = 16
STEPS_PER_TRIP = 4
BAND_SIDE = 3
BAND_CHUNKS = 2 * BAND_SIDE + 1
BAND_DIST = BAND_SIDE * TK + 1
BAND_MARGIN = 160.0
NORM_SLACK = 1.01
ALIBI_SLOPES = tuple(2.0 ** (-8.0 * (i + 1) / N_HEADS) for i in range(N_HEADS))
assert TK == TQ

F32 = jnp.float32
BF16 = jnp.bfloat16
NT_DIMS = (((1,), (1,)), ((), ()))


def _lambda_init(layer):
    return 0.8 - 0.6 * math.exp(-0.3 * layer)


def _sub_rows(ref, i):
    return ref.at[pl.ds(i * TM_PROJ, TM_PROJ)]


def _in_proj_kernel(x_ref, g_ref, w_ref, z_ref):
    for i in range(PROJ_SPLIT):
        _in_proj_rows(_sub_rows(x_ref, i)[...], g_ref, w_ref, _sub_rows(z_ref, i))


def _in_proj_rows(x, g_ref, w_ref, z_ref):
    ms = jnp.mean(x * x, axis=-1, keepdims=True)
    h = (x * lax.rsqrt(ms + EPS) * g_ref[...]).astype(BF16)
    for n0 in range(0, D_IN, D_ATT):
        zc = jnp.dot(h, w_ref[:, n0:n0 + D_ATT], preferred_element_type=F32)
        if n0 == D_ATT:
            zc = zc * LOG2E
        z_ref[:, n0:n0 + D_ATT] = zc.astype(BF16)


def _whole_spec(shape):
    return pl.BlockSpec(shape, lambda r: (0,) * len(shape), pipeline_mode=pl.Buffered(1))


def _in_proj(x2, g, w):
    n = x2.shape[0]
    return pl.pallas_call(
        _in_proj_kernel,
        grid=(n // PROJ_ROWS,),
        in_specs=[
            pl.BlockSpec((PROJ_ROWS, D_MODEL), lambda r: (r, 0)),
            _whole_spec((1, D_MODEL)),
            _whole_spec((D_MODEL, D_IN)),
        ],
        out_specs=pl.BlockSpec((PROJ_ROWS, D_IN), lambda r: (r, 0)),
        out_shape=jax.ShapeDtypeStruct((n, D_IN), BF16),
        compiler_params=pltpu.CompilerParams(
            dimension_semantics=("parallel",), vmem_limit_bytes=VMEM_LIMIT),
        name="in_proj",
    )(x2, g, w)


def _attn_kernel(slope_ref, band_ref, q_ref, k_ref, v_ref, g_ref, lq_ref, sg_ref, o_ref,
                 qq_ref, bias_ref, vone_ref, s0_ref, s1_ref, mx_ref, mrep_ref, acc_ref,
                 *, seq, lam_init):
    s_refs = (s0_ref, s1_ref)
    h = pl.program_id(0)
    nq = seq // TQ
    nk = seq // TK
    group_chunks = max(d for d in range(1, GROUP_CHUNKS + 1) if nk % d == 0)
    kslabs = TK // LANES
    n_bias = (2 * seq - TQ) // LANES

    @pl.when(pl.program_id(1) == 0)
    def _():
        slope = slope_ref[h] * LOG2E
        t = lax.broadcasted_iota(jnp.int32, (TQ, LANES), 0) + (seq - TQ)
        x = lax.broadcasted_iota(jnp.int32, (TQ, LANES), 1)
        base = (t - x).astype(F32)

        def fill(u, carry):
            d = base - jnp.asarray(u * LANES, dtype=F32)
            bias_ref[u] = -slope * jnp.abs(d)
            return carry

        lax.fori_loop(0, n_bias, fill, 0)

    vone_ref[:, 0:HEAD_DIM] = v_ref[...]
    vone_ref[:, HEAD_DIM:2 * HEAD_DIM] = jnp.ones((seq, HEAD_DIM), BF16)

    def reset_stats():
        mx_ref[...] = jnp.full(mx_ref.shape, -jnp.inf, F32)
        acc_ref[...] = jnp.zeros(acc_ref.shape, F32)

    def load_queries(j):
        q = q_ref[pl.ds(pl.multiple_of(j * TQ, TQ), TQ), :]
        lane = lax.broadcasted_iota(jnp.int32, (TQ, HEAD_DIM), 1)
        zero = jnp.zeros_like(q)
        scale = jnp.asarray(QK_DIM ** -0.5, BF16)
        qq_ref[0:TQ, :] = jnp.where(lane < QK_DIM, q, zero) * scale
        qq_ref[TQ:2 * TQ, :] = jnp.where(lane >= QK_DIM, q, zero) * scale

    def score_chunk(j, c, s_ref):
        k0 = pl.multiple_of(c * TK, TK)
        s = lax.dot_general(qq_ref[...], k_ref[pl.ds(k0, TK), :], NT_DIMS,
                            preferred_element_type=F32)
        ub = (seq - TQ) // LANES - j * (TQ // LANES) + c * kslabs
        bias = jnp.concatenate([bias_ref[ub + r] for r in range(kslabs)], axis=1)
        s = s + jnp.concatenate([bias, bias], axis=0)
        s_ref[c] = s
        mx = mx_ref[...]
        for r in range(kslabs):
            mx = jnp.maximum(mx, s[:, r * LANES:(r + 1) * LANES])
        mx_ref[...] = mx

    def prob_chunk(c, s_ref):
        k0 = pl.multiple_of(c * TK, TK)
        s = s_ref[c]
        m = mrep_ref[...]
        p = jnp.exp2(s - jnp.concatenate([m] * kslabs, axis=1))
        return jnp.dot(p.astype(BF16), vone_ref[pl.ds(k0, TK), :],
                       preferred_element_type=F32)

    def publish_max():
        m = jnp.max(mx_ref[...], axis=-1, keepdims=True)
        mrep_ref[...] = jnp.broadcast_to(m, mrep_ref.shape)

    def finalize(j):
        lq = lq_ref[...]
        lam = (jnp.exp(jnp.sum(lq[0:1] * lq[1:2], axis=-1, keepdims=True))
               - jnp.exp(jnp.sum(lq[2:3] * lq[3:4], axis=-1, keepdims=True)) + lam_init)
        o = acc_ref[:, 0:HEAD_DIM] / acc_ref[:, HEAD_DIM:2 * HEAD_DIM]
        o = o[0:TQ] - lam * o[TQ:2 * TQ]
        ms = jnp.mean(o * o, axis=-1, keepdims=True)
        o = o * lax.rsqrt(ms + EPS) * sg_ref[...] * (1.0 - lam_init)
        rows = pl.ds(pl.multiple_of(j * TQ, TQ), TQ)
        g = g_ref[rows, :].astype(F32)
        o_ref[rows, :] = (o * (g * jax.nn.sigmoid(g))).astype(BF16)

    def run_group(fn, chunk_tuples):
        total = None
        for cs in chunk_tuples:
            pv = fn(*cs)
            if pv is not None:
                total = pv if total is None else total + pv
        if total is not None:
            acc_ref[...] += total

    def for_chunks(band, fn, *blocks):
        if band:
            starts = [jnp.clip(jnp.asarray(j, jnp.int32) - BAND_SIDE, 0, nk - BAND_CHUNKS)
                      for j in blocks]
            run_group(fn, [[st + w for st in starts] for w in range(BAND_CHUNKS)])
        else:
            def group(gi, carry):
                run_group(fn, [[gi * group_chunks + w] * len(blocks)
                               for w in range(group_chunks)])
                return carry

            lax.fori_loop(0, nk // group_chunks, group, 0)

    def run(band):
        reset_stats()
        load_queries(0)
        for_chunks(band, lambda c: score_chunk(0, c, s_refs[0]), 0)
        publish_max()
        reset_stats()

        def block_step(j, parity):
            load_queries(j)
            cur, prev = s_refs[parity], s_refs[1 - parity]

            def both(cj, cp):
                pv = prob_chunk(cp, prev)
                score_chunk(j, cj, cur)
                return pv

            if band:
                for_chunks(band, both, j, j - 1)
            else:
                def group(gi, carry):
                    run_group(both, [[gi * group_chunks + w] * 2 for w in range(group_chunks)])
                    return carry

                if nk > group_chunks:
                    lax.fori_loop(0, band_ref[N_HEADS] * (nk // group_chunks - 1), group, 0)
                run_group(both, [[c, c] for c in range(nk - group_chunks, nk)])
            finalize(j - 1)
            publish_max()
            reset_stats()

        def block_steps(t, carry):
            for i in range(STEPS_PER_TRIP):
                block_step(STEPS_PER_TRIP * t + 1 + i, (1 + i) % 2)
            return carry

        trips = (nq - 1) // STEPS_PER_TRIP
        lax.fori_loop(0, trips, block_steps, 0)
        for j in range(STEPS_PER_TRIP * trips + 1, nq):
            block_step(j, j % 2)

        last = s_refs[(nq - 1) % 2]
        for_chunks(band, lambda c: prob_chunk(c, last), nq - 1)
        finalize(nq - 1)

    if nk <= BAND_CHUNKS:
        run(False)
        return

    def band_is_exact():
        sel = (lax.broadcasted_iota(jnp.int32, (HEAD_DIM, 2 * HEAD_DIM), 0) // QK_DIM
               == lax.broadcasted_iota(jnp.int32, (HEAD_DIM, 2 * HEAD_DIM), 1) // HEAD_DIM
               ).astype(BF16)

        def max_half_norm2(ref):
            def body(r, best):
                x = ref[pl.ds(pl.multiple_of(r * TK, TK), TK), :]
                n2 = jnp.dot(x * x, sel, preferred_element_type=F32)
                return jnp.maximum(
                    best, jnp.max(n2.reshape(TK // SUBLANES, SUBLANES, 2 * HEAD_DIM), axis=0))

            best = lax.fori_loop(0, seq // TK, body, jnp.zeros((SUBLANES, 2 * HEAD_DIM), F32),
                                 unroll=True)
            return jnp.max(best, axis=0, keepdims=True)

        n2 = max_half_norm2(q_ref) * max_half_norm2(k_ref) * NORM_SLACK
        slope = jnp.full((1, 2 * HEAD_DIM), slope_ref[h], F32)
        gap = slope * (LOG2E * BAND_DIST) - BAND_MARGIN
        half = gap * (0.5 * QK_DIM ** 0.5)
        violated = jnp.where((n2 <= half * half) & (gap > 0.0), 0, 1)
        return jnp.max(violated)

    violated = lax.cond(band_ref[h] > 0, band_is_exact, lambda: jnp.int32(1))
    lax.cond(violated == 0, lambda: run(True), lambda: run(False))


def _attention(z, lq, sg, *, batch, seq, layer):
    n = batch * seq
    slopes = jnp.asarray(ALIBI_SLOPES, F32)
    band_heads = jnp.asarray(
        [int(sl * LOG2E * BAND_DIST > BAND_MARGIN) for sl in ALIBI_SLOPES] + [1], jnp.int32)
    kernel = functools.partial(_attn_kernel, seq=seq, lam_init=_lambda_init(layer))
    n_bias = (2 * seq - TQ) // LANES
    col = lambda off: (lambda h, b: (b, off + h))
    return pl.pallas_call(
        kernel,
        grid=(N_HEADS, batch),
        in_specs=[
            pl.BlockSpec(memory_space=pltpu.SMEM),
            pl.BlockSpec(memory_space=pltpu.SMEM),
            pl.BlockSpec((seq, HEAD_DIM), col(0)),
            pl.BlockSpec((seq, HEAD_DIM), col(N_HEADS)),
            pl.BlockSpec((seq, HEAD_DIM), col(2 * N_HEADS)),
            pl.BlockSpec((seq, HEAD_DIM), col(3 * N_HEADS)),
            pl.BlockSpec((4, QK_DIM), lambda h, b: (0, 0)),
            pl.BlockSpec((1, HEAD_DIM), lambda h, b: (0, 0)),
        ],
        out_specs=pl.BlockSpec((seq, HEAD_DIM), col(0)),
        out_shape=jax.ShapeDtypeStruct((n, D_ATT), BF16),
        scratch_shapes=[
            pltpu.VMEM((2 * TQ, HEAD_DIM), BF16),
            pltpu.VMEM((n_bias, TQ, LANES), F32),
            pltpu.VMEM((seq, 2 * HEAD_DIM), BF16),
            pltpu.VMEM((seq // TK, 2 * TQ, TK), F32),
            pltpu.VMEM((seq // TK, 2 * TQ, TK), F32),
            pltpu.VMEM((2 * TQ, LANES), F32),
            pltpu.VMEM((2 * TQ, LANES), F32),
            pltpu.VMEM((2 * TQ, 2 * HEAD_DIM), F32),
        ],
        compiler_params=pltpu.CompilerParams(
            dimension_semantics=("parallel", "arbitrary"),
            vmem_limit_bytes=VMEM_LIMIT),
        name="diff_attn",
    )(slopes, band_heads, z, z, z, z, lq, sg)


def _out_rows(x_ref, att_ref, u_ref, vg_ref, gg_ref, vng_ref, ws_ref, bs_ref, wo_ref, sgu_ref):
    vg = vg_ref[...].astype(F32)
    ms = jnp.mean(vg * vg, axis=-1, keepdims=True)
    vn = (vg * lax.rsqrt(ms + EPS) * vng_ref[...]).astype(BF16)
    gg = gg_ref[...].astype(F32)
    gate = u_ref[...].astype(F32) * (gg * jax.nn.sigmoid(gg))
    n_chunks = TM_PROJ // CHUNK
    for g in range(N_GROUPS):
        c0 = g * GROUP_DIM
        vn_g = jnp.concatenate(
            [vn[c * CHUNK:(c + 1) * CHUNK, c0:c0 + GROUP_DIM] for c in range(n_chunks)], axis=1)
        sv = jnp.dot(ws_ref[g], vn_g, preferred_element_type=F32) + bs_ref[g]
        for c in range(n_chunks):
            r0 = c * CHUNK
            sgu_ref[r0:r0 + CHUNK, c0:c0 + GROUP_DIM] = (
                gate[r0:r0 + CHUNK, c0:c0 + GROUP_DIM]
                * sv[:, c * GROUP_DIM:(c + 1) * GROUP_DIM]).astype(BF16)
    y = jnp.dot(att_ref[...], wo_ref[0:D_ATT, :], preferred_element_type=F32)
    y = y + jnp.dot(sgu_ref[...], wo_ref[D_ATT:D_MODEL, :], preferred_element_type=F32)
    return x_ref[...] + y


def _out_final_kernel(x_ref, att_ref, u_ref, vg_ref, gg_ref, vng_ref, ws_ref, bs_ref,
                      wo_ref, fg_ref, o_ref, sgu_ref):
    for i in range(PROJ_SPLIT):
        o = _out_rows(*[_sub_rows(r, i) for r in (x_ref, att_ref, u_ref, vg_ref, gg_ref)],
                      vng_ref, ws_ref, bs_ref, wo_ref, sgu_ref.at[i])
        ms = jnp.mean(o * o, axis=-1, keepdims=True)
        _sub_rows(o_ref, i)[...] = o * lax.rsqrt(ms + EPS) * fg_ref[...]


def _out_in_kernel(x_ref, att_ref, u_ref, vg_ref, gg_ref, vng_ref, ws_ref, bs_ref,
                   wo_ref, g_next_ref, w_next_ref, o_ref, z_ref, sgu_ref):
    for i in range(PROJ_SPLIT):
        o = _out_rows(*[_sub_rows(r, i) for r in (x_ref, att_ref, u_ref, vg_ref, gg_ref)],
                      vng_ref, ws_ref, bs_ref, wo_ref, sgu_ref.at[i])
        _sub_rows(o_ref, i)[...] = o
        _in_proj_rows(o, g_next_ref, w_next_ref, _sub_rows(z_ref, i))


def _out_proj(x2, att, z, vng, ws, bs, wo, g, w_next=None):
    n = x2.shape[0]
    u_blk = 4 * D_ATT // D_GMLP
    rows = lambda width, blk: pl.BlockSpec((PROJ_ROWS, width), lambda r: (r, blk))
    whole = _whole_spec
    in_specs = [
        rows(D_MODEL, 0), rows(D_ATT, 0),
        rows(D_GMLP, u_blk), rows(D_GMLP, u_blk + 1), rows(D_GMLP, u_blk + 2),
        whole((1, D_GMLP)), whole((N_GROUPS, CHUNK, CHUNK)), whole((N_GROUPS, CHUNK, 1)),
        whole((D_MODEL, D_MODEL)), whole((1, D_MODEL)),
    ]
    operands = [x2, att, z, z, z, vng, ws, bs, wo, g]
    out_specs = rows(D_MODEL, 0)
    out_shape = jax.ShapeDtypeStruct((n, D_MODEL), F32)
    body = _out_final_kernel
    if w_next is not None:
        in_specs.append(whole((D_MODEL, D_IN)))
        operands.append(w_next)
        out_specs = (out_specs, rows(D_IN, 0))
        out_shape = (out_shape, jax.ShapeDtypeStruct((n, D_IN), BF16))
        body = _out_in_kernel
    return pl.pallas_call(
        body,
        grid=(n // PROJ_ROWS,),
        in_specs=in_specs,
        out_specs=out_specs,
        out_shape=out_shape,
        scratch_shapes=[pltpu.VMEM((PROJ_SPLIT, TM_PROJ, D_GMLP), BF16)],
        compiler_params=pltpu.CompilerParams(
            dimension_semantics=("parallel",), vmem_limit_bytes=VMEM_LIMIT),
        name="out_proj" if w_next is None else "out_in_proj",
    )(*operands)


def _trunk(x, norm_g, w_in, lambda_qk, subln_g, vnorm_g, w_s, b_s, w_out, final_g):
    batch, seq, _ = x.shape
    x2 = x.reshape(batch * seq, D_MODEL)
    z = _in_proj(x2, norm_g[0].reshape(1, D_MODEL), w_in[0])
    for l in range(DEPTH):
        att = _attention(z, lambda_qk[l], subln_g[l].reshape(1, HEAD_DIM),
                         batch=batch, seq=seq, layer=l)
        layer_args = (x2, att, z, vnorm_g[l].reshape(1, D_GMLP), w_s[l],
                      b_s[l].reshape(N_GROUPS, CHUNK, 1), w_out[l])
        if l + 1 < DEPTH:
            x2, z = _out_proj(*layer_args, norm_g[l + 1].reshape(1, D_MODEL), w_in[l + 1])
        else:
            x2 = _out_proj(*layer_args, final_g.reshape(1, D_MODEL))
    return x2.reshape(batch, seq, D_MODEL)


def kernel(x_prompt, x_sample, norm_g, w_in, lambda_qk, subln_g, vnorm_g, w_s, b_s, w_out, final_g):
    w_in_b = w_in.astype(BF16)
    w_s_b = w_s.astype(BF16)
    w_out_b = w_out.astype(BF16)
    args = (norm_g, w_in_b, lambda_qk, subln_g, vnorm_g, w_s_b, b_s, w_out_b, final_g)
    return (_trunk(x_prompt, *args), _trunk(x_sample, *args))
```
